```python
import jax, jax.numpy as jnp
from jax import lax
import numpy as np

D_MODEL = 2048
BATCH = 4
SEQ = 8192
DEPTH = 2

HEAD_DIM = 128
N_HEADS_SB = 8
N_HEADS_FOX = 8
N_HEADS_DSA = 8
N_KV_DSA = 2
IDX_HEADS = 16
IDX_DIM = 64
TOPK_MAX = 256
Q_BLOCK = 128
ROPE_THETA = 10000.0
NORM_EPS = 1e-6
N_BRANCH = 3
BRANCH_WIDTH = N_HEADS_SB * HEAD_DIM
SB_W = N_HEADS_SB * HEAD_DIM
FOX_W = N_HEADS_FOX * HEAD_DIM
DSA_W = N_HEADS_DSA * HEAD_DIM
DSA_KV_W = N_KV_DSA * HEAD_DIM
D_FF = -(-8 * D_MODEL // (3 * 256)) * 256
IN_SIZES = (SB_W, SB_W, SB_W,
            FOX_W, FOX_W, FOX_W, N_HEADS_FOX,
            DSA_W, DSA_KV_W, DSA_KV_W,
            IDX_HEADS * IDX_DIM, IDX_DIM, IDX_HEADS)
D_IN = sum(IN_SIZES)

kernel_name = "hybrid_sb_fox_dsa_gated_block"


def rmsnorm(x, g):
    xf = x.astype(jnp.float32)
    y = xf * lax.rsqrt(jnp.mean(xf * xf, axis=-1, keepdims=True) + NORM_EPS)
    return (y * g.astype(jnp.float32)).astype(x.dtype)


def rope_tables(seq, dim):
    inv = 1.0 / (ROPE_THETA ** (jnp.arange(0, dim, 2, dtype=jnp.float32) / dim))
    ang = jnp.arange(seq, dtype=jnp.float32)[:, None] * inv[None, :]
    return jnp.cos(ang), jnp.sin(ang)


def apply_rope(x, cos, sin):
    xf = x.astype(jnp.float32)
    x1, x2 = jnp.split(xf, 2, axis=-1)
    c = cos[None, :, None, :]
    s = sin[None, :, None, :]
    return jnp.concatenate([x1 * c - x2 * s, x2 * c + x1 * s], axis=-1).astype(x.dtype)


def to_blocks(a):
    b, s = a.shape[0], a.shape[1]
    return jnp.moveaxis(a.reshape((b, s // Q_BLOCK, Q_BLOCK) + a.shape[2:]), 1, 0)


def from_blocks(a):
    a = jnp.moveaxis(a, 0, 1)
    return a.reshape((a.shape[0], a.shape[1] * a.shape[2]) + a.shape[3:])


def block_starts(seq):
    return jnp.arange(seq // Q_BLOCK, dtype=jnp.int32) * Q_BLOCK


def stick_breaking_attention(q, k, v):
    S, d = q.shape[1], q.shape[3]
    scale = d ** -0.5
    spos = jnp.arange(S, dtype=jnp.int32)

    def block(args):
        qb, t0 = args
        tpos = t0 + jnp.arange(Q_BLOCK, dtype=jnp.int32)
        strict = (spos[None, :] < tpos[:, None])[None, None]
        z = jnp.einsum('bqhd,bshd->bhqs', qb, k).astype(jnp.float32) * scale
        log_keep = jnp.where(strict, jax.nn.log_sigmoid(-z), 0.0)
        log_after = lax.cumsum(log_keep, axis=3, reverse=True) - log_keep
        a = jnp.where(strict, jnp.exp(jax.nn.log_sigmoid(z) + log_after), 0.0)
        return jnp.einsum('bhqs,bshd->bqhd', a.astype(v.dtype), v)

    return from_blocks(lax.map(block, (to_blocks(q), block_starts(S))))


def forgetting_attention(q, k, v, log_f):
    S, d = q.shape[1], q.shape[3]
    scale = d ** -0.5
    spos = jnp.arange(S, dtype=jnp.int32)
    c = jnp.cumsum(log_f, axis=1)
    c_keys = jnp.transpose(c, (0, 2, 1))[:, :, None, :]

    def block(args):
        qb, cb, t0 = args
        tpos = t0 + jnp.arange(Q_BLOCK, dtype=jnp.int32)
        causal = (spos[None, :] <= tpos[:, None])[None, None]
        logits = (jnp.einsum('bqhd,bshd->bhqs', qb, k).astype(jnp.float32) * scale
                  + jnp.transpose(cb, (0, 2, 1))[..., None] - c_keys)
        p = jax.nn.softmax(jnp.where(causal, logits, -jnp.inf), axis=-1)
        return jnp.einsum('bhqs,bshd->bqhd', p.astype(v.dtype), v)

    return from_blocks(lax.map(block, (to_blocks(q), to_blocks(c), block_starts(S))))


def dsa_sparse_attention(q, k, v, iq, ik, iw, topk):
    B, S, H, d = q.shape
    G = k.shape[2]
    R = H // G
    scale = d ** -0.5
    idx_scale = IDX_DIM ** -0.5
    spos = jnp.arange(S, dtype=jnp.int32)

    def block(args):
        qb, iqb, iwb, t0 = args
        tpos = t0 + jnp.arange(Q_BLOCK, dtype=jnp.int32)
        causal = (spos[None, :] <= tpos[:, None])[None]
        rel = jax.nn.relu(jnp.einsum('bqhe,bse->bqhs', iqb, ik).astype(jnp.float32) * idx_scale)
        score = jnp.einsum('bqh,bqhs->bqs', iwb.astype(jnp.float32), rel)
        score = jnp.where(causal, score, -jnp.inf)
        _, sel = lax.top_k(score, topk)
        k_sel = jax.vmap(lambda kb, ib: kb[ib])(k, sel)
        v_sel = jax.vmap(lambda vb, ib: vb[ib])(v, sel)
        valid = (sel <= tpos[None, :, None])[:, :, None, None, :]
        qg = qb.reshape(B, Q_BLOCK, G, R, d)
        logits = jnp.einsum('bqgrd,bqkgd->bqgrk', qg, k_sel).astype(jnp.float32) * scale
        p = jax.nn.softmax(jnp.where(valid, logits, -jnp.inf), axis=-1)
        o = jnp.einsum('bqgrk,bqkgd->bqgrd', p.astype(v.dtype), v_sel)
        return o.reshape(B, Q_BLOCK, H, d)

    return from_blocks(lax.map(block, (to_blocks(q), to_blocks(iq), to_blocks(iw), block_starts(S))))


def hybrid_layer(x, cos_h, sin_h, cos_i, sin_i, topk,
                 norm_mix_g, w_in, fox_f_bias, fox_q_g, fox_k_g, dsa_q_g, dsa_k_g,
                 w_gate, w_branch, w_out, norm_ffn_g, w_ffn_gate, w_ffn_up, w_ffn_down):
    B, S, _ = x.shape
    h = rmsnorm(x, norm_mix_g)
    points = np.cumsum(IN_SIZES)[:-1].tolist()
    (sb_q, sb_k, sb_v, fox_q, fox_k, fox_v, fox_f,
     dsa_q, dsa_k, dsa_v, idx_q, idx_k, idx_w) = jnp.split(h @ w_in, points, axis=-1)
    heads = lambda a, n: a.reshape(B, S, n, -1)

    o_sb = stick_breaking_attention(heads(sb_q, N_HEADS_SB), heads(sb_k, N_HEADS_SB),
                                    heads(sb_v, N_HEADS_SB))

    log_f = jax.nn.log_sigmoid(fox_f.astype(jnp.float32) + fox_f_bias.astype(jnp.float32))
    o_fox = forgetting_attention(rmsnorm(heads(fox_q, N_HEADS_FOX), fox_q_g),
                                 rmsnorm(heads(fox_k, N_HEADS_FOX), fox_k_g),
                                 heads(fox_v, N_HEADS_FOX), log_f)

    qc = apply_rope(rmsnorm(heads(dsa_q, N_HEADS_DSA), dsa_q_g), cos_h, sin_h)
    kc = apply_rope(rmsnorm(heads(dsa_k, N_KV_DSA), dsa_k_g), cos_h, sin_h)
    iq = apply_rope(heads(idx_q, IDX_HEADS), cos_i, sin_i)
    ik = apply_rope(idx_k[:, :, None, :], cos_i, sin_i)[:, :, 0, :]
    iw = idx_w * (IDX_HEADS ** -0.5)
    o_dsa = dsa_sparse_attention(qc, kc, heads(dsa_v, N_KV_DSA), iq, ik, iw, topk)

    merged = None
    for i, o in enumerate((o_sb, o_fox, o_dsa)):
        y = jax.nn.sigmoid(h @ w_gate[i]) * (o.reshape(B, S, BRANCH_WIDTH) @ w_branch[i])
        merged = y if merged is None else merged + y
    x = x + merged @ w_out

    h2 = rmsnorm(x, norm_ffn_g)
    return x + (jax.nn.silu(h2 @ w_ffn_gate) * (h2 @ w_ffn_up)) @ w_ffn_down


def setup_inputs(seed: int = 0) -> dict:
    key = jax.random.key(seed)
    ks = jax.random.split(key, 16)
    f32 = jnp.float32

    def dense(k, shape, fan_in):
        return jax.random.normal(k, shape, f32) * (fan_in ** -0.5)

    def gain(k, shape):
        return 1.0 + 0.02 * jax.random.normal(k, shape, f32)

    return {
        "x": jax.random.normal(ks[0], (BATCH, SEQ, D_MODEL), f32),
        "norm_mix_g": gain(ks[1], (DEPTH, D_MODEL)),
        "w_in": dense(ks[2], (DEPTH, D_MODEL, D_IN), D_MODEL),
        "fox_f_bias": jax.random.uniform(ks[3], (DEPTH, N_HEADS_FOX), f32, 1.0, 4.0),
        "fox_q_g": gain(ks[4], (DEPTH, HEAD_DIM)),
        "fox_k_g": gain(ks[5], (DEPTH, HEAD_DIM)),
        "dsa_q_g": gain(ks[6], (DEPTH, HEAD_DIM)),
        "dsa_k_g": gain(ks[7], (DEPTH, HEAD_DIM)),
        "w_gate": dense(ks[8], (DEPTH, N_BRANCH, D_MODEL, D_MODEL), D_MODEL),
        "w_branch": dense(ks[9], (DEPTH, N_BRANCH, BRANCH_WIDTH, D_MODEL), BRANCH_WIDTH),
        "w_out": dense(ks[10], (DEPTH, D_MODEL, D_MODEL), D_MODEL),
        "norm_ffn_g": gain(ks[11], (DEPTH, D_MODEL)),
        "w_ffn_gate": dense(ks[12], (DEPTH, D_MODEL, D_FF), D_MODEL),
        "w_ffn_up": dense(ks[13], (DEPTH, D_MODEL, D_FF), D_MODEL),
        "w_ffn_down": dense(ks[14], (DEPTH, D_FF, D_MODEL), D_FF),
    }


def reference(x, norm_mix_g, w_in, fox_f_bias, fox_q_g, fox_k_g, dsa_q_g, dsa_k_g,
              w_gate, w_branch, w_out, norm_ffn_g, w_ffn_gate, w_ffn_up, w_ffn_down):
    S = x.shape[1]
    topk = min(TOPK_MAX, S // 4)
    cos_h, sin_h = rope_tables(S, HEAD_DIM)
    cos_i, sin_i = rope_tables(S, IDX_DIM)
    for l in range(DEPTH):
        x = hybrid_layer(x, cos_h, sin_h, cos_i, sin_i, topk,
                         norm_mix_g[l], w_in[l], fox_f_bias[l], fox_q_g[l], fox_k_g[l],
                         dsa_q_g[l], dsa_k_g[l], w_gate[l], w_branch[l], w_out[l],
                         norm_ffn_g[l], w_ffn_gate[l], w_ffn_up[l], w_ffn_down[l])
    return x
```

```python
import functools

import jax
import jax.numpy as jnp
from jax import lax
from jax.experimental import pallas as pl
from jax.experimental.pallas import tpu as pltpu

F32 = jnp.float32
BF16 = jnp.bfloat16

HEAD_DIM = 128
N_HEADS = 8
N_KV_DSA = 2
IDX_HEADS = 16
IDX_DIM = 64
TOPK_MAX = 256
ROPE_THETA = 10000.0
NORM_EPS = 1e-6
BRANCH_W = N_HEADS * HEAD_DIM
MISC_W = 128
PROJ_W = 3 * BRANCH_W + 3 * BRANCH_W + (BRANCH_W + 2 * N_KV_DSA * HEAD_DIM) + IDX_HEADS * IDX_DIM

NEG = -1e30
INT_MIN = -(2 ** 31)
SB_SKIP = 100.0
VMEM_LIMIT_BYTES = 56 * 1024 * 1024


def _cparams(sem):
    return pltpu.CompilerParams(dimension_semantics=sem, vmem_limit_bytes=VMEM_LIMIT_BYTES)


def _tile(n, pref):
    t = pref
    while n % t:
        t //= 2
    return t


def _dot(a, b):
    return jnp.dot(a, b, preferred_element_type=F32)


def _dot_nt(a, b):
    return lax.dot_general(a, b, (((1,), (1,)), ((), ())), preferred_element_type=F32)


def _rep(x, n):
    return x if n == 1 else jnp.concatenate([x] * n, axis=1)


def _rmsnorm_f32(x, g):
    return x * lax.rsqrt(jnp.mean(x * x, axis=-1, keepdims=True) + NORM_EPS) * g


def _trunc_bf16(x):
    bits = lax.bitcast_convert_type(x, jnp.uint32) & jnp.uint32(0xFFFF0000)
    return lax.bitcast_convert_type(bits, F32)


def _norm_proj_kernel(x_ref, g_ref, w_ref, wmh_ref, wml_ref, proj_ref, h_ref, misc_ref, hs_ref):
    @pl.when(pl.program_id(1) == 0)
    def _():
        y = _rmsnorm_f32(x_ref[...], g_ref[...])
        hb = y.astype(BF16)
        hs_ref[...] = hb
        h_ref[...] = hb
        y_hi = _trunc_bf16(y)
        hi = y_hi.astype(BF16)
        lo = (y - y_hi).astype(BF16)
        misc_ref[...] = (_dot(hi, wmh_ref[...]) + _dot(hi, wml_ref[...])) + _dot(lo, wmh_ref[...])

    proj_ref[...] = _dot(hs_ref[...], w_ref[...]).astype(proj_ref.dtype)


def norm_proj(x, g, w, wm_hi, wm_lo):
    T, D = x.shape
    N = w.shape[1]
    tm, tn = _tile(T, 512), _tile(N, 512)
    return pl.pallas_call(
        _norm_proj_kernel,
        grid=(T // tm, N // tn),
        in_specs=[
            pl.BlockSpec((tm, D), lambda i, j: (i, 0)),
            pl.BlockSpec((1, D), lambda i, j: (0, 0)),
            pl.BlockSpec((D, tn), lambda i, j: (0, j)),
            pl.BlockSpec((D, MISC_W), lambda i, j: (0, 0)),
            pl.BlockSpec((D, MISC_W), lambda i, j: (0, 0)),
        ],
        out_specs=[
            pl.BlockSpec((tm, tn), lambda i, j: (i, j)),
            pl.BlockSpec((tm, D), lambda i, j: (i, 0)),
            pl.BlockSpec((tm, MISC_W), lambda i, j: (i, 0)),
        ],
        out_shape=[
            jax.ShapeDtypeStruct((T, N), BF16),
            jax.ShapeDtypeStruct((T, D), BF16),
            jax.ShapeDtypeStruct((T, MISC_W), F32),
        ],
        scratch_shapes=[pltpu.VMEM((tm, D), BF16)],
        compiler_params=_cparams(("parallel", "arbitrary")),
        name="norm_proj",
    )(x, g.reshape(1, D), w, wm_hi, wm_lo)


def _merge_kernel(h_ref, o0_ref, o1_ref, o2_ref, wg_ref, wb_ref, out_ref):
    h = h_ref[...]
    acc = None
    for i, o_ref in enumerate((o0_ref, o1_ref, o2_ref)):
        y = jax.nn.sigmoid(_dot(h, wg_ref[i])) * _dot(o_ref[...], wb_ref[i])
        acc = y if acc is None else acc + y
    out_ref[...] = acc.astype(out_ref.dtype)


def merge_branches(h, o_sb, o_fox, o_dsa, wg, wb):
    T, D = h.shape
    W = o_sb.shape[1]
    N = wg.shape[2]
    tm, tn = _tile(T, 512), _tile(N, 512)
    o_spec = pl.BlockSpec((tm, W), lambda i, j: (i, 0))
    return pl.pallas_call(
        _merge_kernel,
        grid=(T // tm, N // tn),
        in_specs=[
            pl.BlockSpec((tm, D), lambda i, j: (i, 0)),
            o_spec, o_spec, o_spec,
            pl.BlockSpec((3, D, tn), lambda i, j: (0, 0, j)),
            pl.BlockSpec((3, W, tn), lambda i, j: (0, 0, j)),
        ],
        out_specs=pl.BlockSpec((tm, tn), lambda i, j: (i, j)),
        out_shape=jax.ShapeDtypeStruct((T, N), BF16),
        compiler_params=_cparams(("parallel", "arbitrary")),
        name="merge_branches",
    )(h, o_sb, o_fox, o_dsa, wg, wb)


def _matmul_residual_kernel(a_ref, b_ref, r_ref, o_ref):
    o_ref[...] = r_ref[...] + _dot(a_ref[...], b_ref[...])


def matmul_residual(a, b, r):
    T, K = a.shape
    N = b.shape[1]
    tm, tn = _tile(T, 512), _tile(N, 512)
    return pl.pallas_call(
        _matmul_residual_kernel,
        grid=(T // tm, N // tn),
        in_specs=[
            pl.BlockSpec((tm, K), lambda i, j: (i, 0)),
            pl.BlockSpec((K, tn), lambda i, j: (0, j)),
            pl.BlockSpec((tm, tn), lambda i, j: (i, j)),
        ],
        out_specs=pl.BlockSpec((tm, tn), lambda i, j: (i, j)),
        out_shape=jax.ShapeDtypeStruct((T, N), F32),
        compiler_params=_cparams(("parallel", "arbitrary")),
        name="matmul_residual",
    )(a, b, r)


def _ffn_up_kernel(x_ref, g_ref, wg_ref, wu_ref, u_ref, hs_ref):
    @pl.when(pl.program_id(1) == 0)
    def _():
        hs_ref[...] = _rmsnorm_f32(x_ref[...], g_ref[...]).astype(BF16)

    h = hs_ref[...]
    u_ref[...] = (jax.nn.silu(_dot(h, wg_ref[...])) * _dot(h, wu_ref[...])).astype(u_ref.dtype)


def ffn_up(x, g, wg, wu):
    T, D = x.shape
    Fd = wg.shape[1]
    tm, tn = _tile(T, 512), _tile(Fd, 512)
    return pl.pallas_call(
        _ffn_up_kernel,
        grid=(T // tm, Fd // tn),
        in_specs=[
            pl.BlockSpec((tm, D), lambda i, j: (i, 0)),
            pl.BlockSpec((1, D), lambda i, j: (0, 0)),
            pl.BlockSpec((D, tn), lambda i, j: (0, j)),
            pl.BlockSpec((D, tn), lambda i, j: (0, j)),
        ],
        out_specs=pl.BlockSpec((tm, tn), lambda i, j: (i, j)),
        out_shape=jax.ShapeDtypeStruct((T, Fd), BF16),
        scratch_shapes=[pltpu.VMEM((tm, D), BF16)],
        compiler_params=_cparams(("parallel", "arbitrary")),
        name="ffn_up",
    )(x, g.reshape(1, D), wg, wu)


def _online_softmax_update(s, v, m_ref, l_ref, acc_ref, idx):
    m_prev = m_ref[idx]
    m_new = jnp.maximum(m_prev, jnp.max(s, axis=1, keepdims=True))
    p = jnp.exp(s - _rep(m_new, s.shape[1] // 128))
    alpha = jnp.exp(m_prev - m_new)
    l_ref[idx] = alpha * l_ref[idx] + jnp.sum(p, axis=1, keepdims=True)
    acc_ref[idx] = alpha * acc_ref[idx] + _dot(p.astype(BF16), v)
    m_ref[idx] = m_new


def _sb_kernel(q_ref, k_ref, v_ref, o_ref, acc_ref, car_ref, *, t):
    i = pl.program_id(2)
    q = q_ref[0]
    row = lax.broadcasted_iota(jnp.int32, (t, t), 0)
    col = lax.broadcasted_iota(jnp.int32, (t, t), 1)
    upper = (row > col).astype(BF16)
    ones = jnp.ones((t, 128), BF16)
    strict = col < row

    def process(j, diag):
        start = pl.multiple_of(j * t, t)
        k = k_ref[0, pl.ds(start, t), :]
        v = v_ref[0, pl.ds(start, t), :]
        z = _dot_nt(q, k)
        log_keep = -(jnp.maximum(z, 0.0) + jnp.log(1.0 + jnp.exp(-jnp.abs(z))))
        lk = jnp.where(strict, log_keep, 0.0) if diag else log_keep
        hi_f = _trunc_bf16(lk)
        hi = hi_f.astype(BF16)
        lo = (lk - hi_f).astype(BF16)
        after = _dot(hi, upper) + _dot(lo, upper)
        total = _dot(hi, ones) + _dot(lo, ones)
        car = car_ref[...]
        a = jnp.exp(z + log_keep + after + _rep(car, t // 128))
        if diag:
            a = jnp.where(strict, a, 0.0)
        acc_ref[...] += _dot(a.astype(BF16), v)
        car = car + total
        car_ref[...] = car
        return jnp.max(car)

    acc_ref[...] = jnp.zeros_like(acc_ref)
    car_ref[...] = jnp.zeros_like(car_ref)
    mx = process(i, True)

    def cond(st):
        j, mx = st
        return jnp.logical_and(j >= 0, mx > -SB_SKIP)

    def body(st):
        j, _ = st
        return j - 1, process(j, False)

    lax.while_loop(cond, body, (i - 1, mx))
    o_ref[0] = acc_ref[...].astype(o_ref.dtype)


def sb_attention(proj3, q_col, k_col, v_col):
    B, S, _ = proj3.shape
    t = _tile(S, 256)
    return pl.pallas_call(
        functools.partial(_sb_kernel, t=t),
        grid=(B, N_HEADS, S // t),
        in_specs=[
            pl.BlockSpec((1, t, HEAD_DIM), lambda b, h, i: (b, i, q_col + h)),
            pl.BlockSpec((1, S, HEAD_DIM), lambda b, h, i: (b, 0, k_col + h)),
            pl.BlockSpec((1, S, HEAD_DIM), lambda b, h, i: (b, 0, v_col + h)),
        ],
        out_specs=pl.BlockSpec((1, t, HEAD_DIM), lambda b, h, i: (b, i, h)),
        out_shape=jax.ShapeDtypeStruct((B, S, BRANCH_W), BF16),
        scratch_shapes=[pltpu.VMEM((t, HEAD_DIM), F32), pltpu.VMEM((t, HEAD_DIM), F32)],
        compiler_params=_cparams(("parallel", "parallel", "arbitrary")),
        name="sb_attention",
    )(proj3, proj3, proj3)


def _fox_kernel(q_ref, k_ref, v_ref, o_ref, m_ref, l_ref, acc_ref, *, t):
    i = pl.program_id(2)
    q = q_ref[0]
    m_ref[...] = jnp.full_like(m_ref, NEG)
    l_ref[...] = jnp.zeros_like(l_ref)
    acc_ref[...] = jnp.zeros_like(acc_ref)
    row = lax.broadcasted_iota(jnp.int32, (t, t), 0)
    col = lax.broadcasted_iota(jnp.int32, (t, t), 1)

    def step(j, masked):
        start = pl.multiple_of(j * t, t)
        s = _dot_nt(q, k_ref[0, pl.ds(start, t), :])
        if masked:
            s = jnp.where(row >= col, s, NEG)
        _online_softmax_update(s, v_ref[0, pl.ds(start, t), :], m_ref, l_ref, acc_ref, 0)

    def body(j, c):
        step(j, False)
        return c

    lax.fori_loop(0, i, body, 0)
    step(i, True)
    o_ref[0] = (acc_ref[0] / l_ref[0]).astype(o_ref.dtype)


def fox_attention(q_ext, k_ext, proj3, v_col):
    B, S, _ = q_ext.shape
    t = _tile(S, 256)
    E = 2 * HEAD_DIM
    return pl.pallas_call(
        functools.partial(_fox_kernel, t=t),
        grid=(B, N_HEADS, S // t),
        in_specs=[
            pl.BlockSpec((1, t, E), lambda b, h, i: (b, i, h)),
            pl.BlockSpec((1, S, E), lambda b, h, i: (b, 0, h)),
            pl.BlockSpec((1, S, HEAD_DIM), lambda b, h, i: (b, 0, v_col + h)),
        ],
        out_specs=pl.BlockSpec((1, t, HEAD_DIM), lambda b, h, i: (b, i, h)),
        out_shape=jax.ShapeDtypeStruct((B, S, BRANCH_W), BF16),
        scratch_shapes=[pltpu.VMEM((1, t, HEAD_DIM), F32)] * 3,
        compiler_params=_cparams(("parallel", "parallel", "arbitrary")),
        name="fox_attention",
    )(q_ext, k_ext, proj3)


def _dsa_kernel(iqs_ref, w_ref, ik_ref, qc_ref, kc_ref, vc_ref, o_ref,
                key_ref, wrep_ref, m_ref, l_ref, acc_ref, *, tq, tk, topk):
    i = pl.program_id(1)
    nkb = (i * tq + tq + tk - 1) // tk
    rep = N_HEADS // N_KV_DSA

    w = w_ref[0]
    for h in range(IDX_HEADS):
        wrep_ref[h] = jnp.broadcast_to(w[:, h:h + 1], (tq, tk))
    iqs = iqs_ref[0, 0]
    rowpos = i * tq + lax.broadcasted_iota(jnp.int32, (tq, tk), 0)
    colio = lax.broadcasted_iota(jnp.int32, (tq, tk), 1)

    def score_block(jk, c):
        start = pl.multiple_of(jk * tk, tk)
        z = _dot_nt(iqs, ik_ref[0, pl.ds(start, tk), :]).reshape(IDX_HEADS, tq, tk)
        score = jnp.sum(jnp.maximum(z, 0.0) * wrep_ref[...], axis=0)
        bits = lax.bitcast_convert_type(score, jnp.int32)
        key = bits ^ ((bits >> 31) & jnp.int32(0x7FFFFFFF))
        key_ref[jk] = jnp.where(start + colio <= rowpos, key, jnp.int32(INT_MIN))
        return c

    lax.fori_loop(0, nkb, score_block, 0)

    def bit_body(it, tau):
        cand = tau + lax.shift_left(jnp.int32(1), 31 - it)
        candw = _rep(cand, tk // 128)

        def count_block(jk, c):
            return c + jnp.where(key_ref[jk] >= candw, 1.0, 0.0)

        cnt = jnp.sum(lax.fori_loop(0, nkb, count_block, jnp.zeros((tq, tk), F32)),
                      axis=1, keepdims=True)
        return jnp.where(cnt >= float(topk), cand, tau)

    tau = lax.fori_loop(0, 32, bit_body, jnp.full((tq, 128), INT_MIN, jnp.int32))
    tauw = _rep(jnp.maximum(tau, jnp.int32(INT_MIN + 1)), tk // 128)

    m_ref[...] = jnp.full_like(m_ref, NEG)
    l_ref[...] = jnp.zeros_like(l_ref)
    acc_ref[...] = jnp.zeros_like(acc_ref)
    qc = qc_ref[0]
    qg = [jnp.concatenate([qc[:, (rep * g + r) * HEAD_DIM:(rep * g + r + 1) * HEAD_DIM]
                           for r in range(rep)], axis=0) for g in range(N_KV_DSA)]

    def attend_block(jk, c):
        start = pl.multiple_of(jk * tk, tk)
        bias = jnp.where(key_ref[jk] >= tauw, 0.0, NEG)
        bias = jnp.concatenate([bias] * rep, axis=0)
        for g in range(N_KV_DSA):
            kg = kc_ref[0, pl.ds(start, tk), g * HEAD_DIM:(g + 1) * HEAD_DIM]
            vg = vc_ref[0, pl.ds(start, tk), g * HEAD_DIM:(g + 1) * HEAD_DIM]
            _online_softmax_update(_dot_nt(qg[g], kg) + bias, vg, m_ref, l_ref, acc_ref, g)
        return c

    lax.fori_loop(0, nkb, attend_block, 0)
    for g in range(N_KV_DSA):
        og = (acc_ref[g] / l_ref[g]).astype(o_ref.dtype)
        for r in range(rep):
            hcol = (rep * g + r) * HEAD_DIM
            o_ref[0, :, hcol:hcol + HEAD_DIM] = og[r * tq:(r + 1) * tq]


def dsa_attention(iqs, iw, ik, qc, kc, proj3, v_col, topk):
    B, S, _ = qc.shape
    tq = _tile(S, 128)
    tk = _tile(S, 256)
    kvw = N_KV_DSA * HEAD_DIM
    rows = (N_HEADS // N_KV_DSA) * tq
    return pl.pallas_call(
        functools.partial(_dsa_kernel, tq=tq, tk=tk, topk=topk),
        grid=(B, S // tq),
        in_specs=[
            pl.BlockSpec((1, 1, IDX_HEADS * tq, IDX_DIM), lambda b, i: (b, i, 0, 0)),
            pl.BlockSpec((1, tq, IDX_HEADS), lambda b, i: (b, i, 0)),
            pl.BlockSpec((1, S, IDX_DIM), lambda b, i: (b, 0, 0)),
            pl.BlockSpec((1, tq, BRANCH_W), lambda b, i: (b, i, 0)),
            pl.BlockSpec((1, S, kvw), lambda b, i: (b, 0, 0)),
            pl.BlockSpec((1, S, kvw), lambda b, i: (b, 0, v_col)),
        ],
        out_specs=pl.BlockSpec((1, tq, BRANCH_W), lambda b, i: (b, i, 0)),
        out_shape=jax.ShapeDtypeStruct((B, S, BRANCH_W), BF16),
        scratch_shapes=[
            pltpu.VMEM((S // tk, tq, tk), jnp.int32),
            pltpu.VMEM((IDX_HEADS, tq, tk), F32),
            pltpu.VMEM((N_KV_DSA, rows, HEAD_DIM), F32),
            pltpu.VMEM((N_KV_DSA, rows, HEAD_DIM), F32),
            pltpu.VMEM((N_KV_DSA, rows, HEAD_DIM), F32),
        ],
        compiler_params=_cparams(("parallel", "arbitrary")),
        name="dsa_attention",
    )(iqs, iw, ik, qc, kc, proj3)


def _rope_tables(seq, dim):
    inv = 1.0 / (ROPE_THETA ** (jnp.arange(0, dim, 2, dtype=F32) / dim))
    ang = jnp.arange(seq, dtype=F32)[:, None] * inv[None, :]
    return jnp.cos(ang), jnp.sin(ang)


def _rope(x, cos, sin):
    x1, x2 = jnp.split(x, 2, axis=-1)
    c, s = cos[None, :, None, :], sin[None, :, None, :]
    return jnp.concatenate([x1 * c - x2 * s, x2 * c + x1 * s], axis=-1)


def _head_norm(x, g):
    return x * lax.rsqrt(jnp.mean(x * x, axis=-1, keepdims=True) + NORM_EPS) * g


def _split3(c):
    c1 = _trunc_bf16(c)
    r = c - c1
    c2 = _trunc_bf16(r)
    return c1.astype(BF16), c2.astype(BF16), (r - c2).astype(BF16)


def _pack_in_weights(w_in):
    D = w_in.shape[0]
    scale = HEAD_DIM ** -0.5
    o = 0
    sb = w_in[:, o:o + 3 * BRANCH_W]; o += 3 * BRANCH_W
    fox = w_in[:, o:o + 3 * BRANCH_W]; o += 3 * BRANCH_W
    fox_f = w_in[:, o:o + N_HEADS]; o += N_HEADS
    dsa_w = BRANCH_W + 2 * N_KV_DSA * HEAD_DIM
    dsa = w_in[:, o:o + dsa_w]; o += dsa_w
    idx_q = w_in[:, o:o + IDX_HEADS * IDX_DIM]; o += IDX_HEADS * IDX_DIM
    idx_k = w_in[:, o:o + IDX_DIM]; o += IDX_DIM
    idx_w = w_in[:, o:o + IDX_HEADS]; o += IDX_HEADS
    sb = jnp.concatenate([sb[:, :BRANCH_W] * scale, sb[:, BRANCH_W:]], axis=1)
    w_main = jnp.concatenate([sb, fox, dsa, idx_q], axis=1).astype(BF16)
    pad = jnp.zeros((D, MISC_W - IDX_DIM - IDX_HEADS - N_HEADS), F32)
    misc = jnp.concatenate([idx_k, idx_w, fox_f, pad], axis=1)
    m_hi = _trunc_bf16(misc)
    return w_main, m_hi.astype(BF16), (misc - m_hi).astype(BF16)


def _layer(x2, B, S, rope_h, rope_i, topk, norm_mix_g, w_in, fox_f_bias, fox_q_g, fox_k_g,
           dsa_q_g, dsa_k_g, w_gate, w_branch, w_out, norm_ffn_g, w_ffn_gate, w_ffn_up, w_ffn_down):
    T, D = x2.shape
    scale = HEAD_DIM ** -0.5
    w_main, wm_hi, wm_lo = _pack_in_weights(w_in)
    proj, h, misc = norm_proj(x2, norm_mix_g, w_main, wm_hi, wm_lo)
    proj3 = proj.reshape(B, S, PROJ_W)
    misc = misc.reshape(B, S, MISC_W)
    nblk = BRANCH_W // HEAD_DIM

    o_sb = sb_attention(proj3, 0, nblk, 2 * nblk)

    fq = proj3[..., 3 * BRANCH_W:4 * BRANCH_W].astype(F32).reshape(B, S, N_HEADS, HEAD_DIM)
    fk = proj3[..., 4 * BRANCH_W:5 * BRANCH_W].astype(F32).reshape(B, S, N_HEADS, HEAD_DIM)
    fq = _head_norm(fq, fox_q_g * scale).astype(BF16)
    fk = _head_norm(fk, fox_k_g).astype(BF16)
    fox_f = misc[..., IDX_DIM + IDX_HEADS:IDX_DIM + IDX_HEADS + N_HEADS]
    c = jnp.cumsum(jax.nn.log_sigmoid(fox_f + fox_f_bias), axis=1)
    c1, c2, c3 = _split3(c)
    one = jnp.ones_like(c1)
    zpad = jnp.zeros((B, S, N_HEADS, HEAD_DIM - 6), BF16)
    st = lambda parts: jnp.stack(parts, axis=-1)
    q_ext = jnp.concatenate([fq, st([c1, c2, c3, one, one, one]), zpad], axis=-1)
    k_ext = jnp.concatenate([fk, st([one, one, one, -c1, -c2, -c3]), zpad], axis=-1)
    o_fox = fox_attention(q_ext.reshape(B, S, -1), k_ext.reshape(B, S, -1), proj3, 5 * nblk)

    cos_h, sin_h = rope_h
    cos_i, sin_i = rope_i
    o = 6 * BRANCH_W
    dq = proj3[..., o:o + BRANCH_W].astype(F32).reshape(B, S, N_HEADS, HEAD_DIM)
    dk = proj3[..., o + BRANCH_W:o + BRANCH_W + N_KV_DSA * HEAD_DIM].astype(F32)
    dk = dk.reshape(B, S, N_KV_DSA, HEAD_DIM)
    qc = _rope(_head_norm(dq, dsa_q_g * scale), cos_h, sin_h).astype(BF16).reshape(B, S, BRANCH_W)
    kc = _rope(_head_norm(dk, dsa_k_g), cos_h, sin_h).astype(BF16).reshape(B, S, -1)
    o_iq = o + BRANCH_W + 2 * N_KV_DSA * HEAD_DIM
    iq = proj3[..., o_iq:o_iq + IDX_HEADS * IDX_DIM].astype(F32).reshape(B, S, IDX_HEADS, IDX_DIM)
    iq = _rope(iq, cos_i, sin_i).astype(BF16)
    tq = _tile(S, 128)
    iqs = iq.reshape(B, S // tq, tq, IDX_HEADS, IDX_DIM).transpose(0, 1, 3, 2, 4)
    iqs = iqs.reshape(B, S // tq, IDX_HEADS * tq, IDX_DIM)
    ik = _rope(misc[..., None, :IDX_DIM], cos_i, sin_i)[:, :, 0, :].astype(BF16)
    iw = misc[..., IDX_DIM:IDX_DIM + IDX_HEADS] * (IDX_HEADS ** -0.5 * IDX_DIM ** -0.5)
    v_col = (o + BRANCH_W + N_KV_DSA * HEAD_DIM) // (N_KV_DSA * HEAD_DIM)
    o_dsa = dsa_attention(iqs, iw, ik, qc, kc, proj3, v_col, topk)

    merged = merge_branches(h, o_sb.reshape(T, -1), o_fox.reshape(T, -1), o_dsa.reshape(T, -1),
                            w_gate.astype(BF16), w_branch.astype(BF16))
    x2 = matmul_residual(merged, w_out.astype(BF16), x2)
    u = ffn_up(x2, norm_ffn_g, w_ffn_gate.astype(BF16), w_ffn_up.astype(BF16))
    return matmul_residual(u, w_ffn_down.astype(BF16), x2)


def kernel(x, norm_mix_g, w_in, fox_f_bias, fox_q_g, fox_k_g, dsa_q_g, dsa_k_g, w_gate, w_branch,
           w_out, norm_ffn_g, w_ffn_gate, w_ffn_up, w_ffn_down):
    B, S, D = x.shape
    topk = min(TOPK_MAX, S // 4)
    rope_h = _rope_tables(S, HEAD_DIM)
    rope_i = _rope_tables(S, IDX_DIM)
    x2 = x.reshape(B * S, D)
    for l in range(norm_mix_g.shape[0]):
        x2 = _layer(x2, B, S, rope_h, rope_i, topk, norm_mix_g[l], w_in[l], fox_f_bias[l],
                    fox_q_g[l], fox_k_g[l], dsa_q_g[l], dsa_k_g[l], w_gate[l], w_branch[l],
                    w_out[l], norm_ffn_g[l], w_ffn_gate[l], w_ffn_up[l], w_ffn_down[l])
    return x2.reshape(B, S, D)
```

```python
import functools

import jax
import jax.numpy as jnp
from jax import lax
from jax.experimental import pallas as pl
from jax.experimental.pallas import tpu as pltpu

F32 = jnp.float32
BF16 = jnp.bfloat16

HEAD_DIM = 128
N_HEADS = 8
N_KV_DSA = 2
IDX_HEADS = 16
IDX_DIM = 64
TOPK_MAX = 256
ROPE_THETA = 10000.0
NORM_EPS = 1e-6
BRANCH_W = N_HEADS * HEAD_DIM
MISC_W = 128
PROJ_W = 3 * BRANCH_W + 3 * BRANCH_W + (BRANCH_W + 2 * N_KV_DSA * HEAD_DIM) + IDX_HEADS * IDX_DIM

LOG2E = 1.4426950408889634
FOX_TQ, FOX_TK = 1024, 512
DSA_TQ, DSA_TK = 256, 512
IDX_GROUP = 4
NEG = -1e30
INT_MIN = -(2 ** 31)
SB_SKIP = 100.0
VMEM_LIMIT_BYTES = 56 * 1024 * 1024


def _cparams(sem):
    return pltpu.CompilerParams(dimension_semantics=sem, vmem_limit_bytes=VMEM_LIMIT_BYTES)


def _tile(n, pref):
    t = pref
    while n % t:
        t //= 2
    return t


def _dot(a, b):
    return jnp.dot(a, b, preferred_element_type=F32)


def _dot_nt(a, b):
    return lax.dot_general(a, b, (((1,), (1,)), ((), ())), preferred_element_type=F32)


def _rep(x, n):
    return x if n == 1 else jnp.concatenate([x] * n, axis=1)


def _rmsnorm_f32(x, g):
    return x * lax.rsqrt(jnp.mean(x * x, axis=-1, keepdims=True) + NORM_EPS) * g


def _trunc_bf16(x):
    bits = lax.bitcast_convert_type(x, jnp.uint32) & jnp.uint32(0xFFFF0000)
    return lax.bitcast_convert_type(bits, F32)


def _norm_proj_kernel(x_ref, g_ref, w_ref, wmh_ref, wml_ref, proj_ref, h_ref, misc_ref, hs_ref):
    @pl.when(pl.program_id(1) == 0)
    def _():
        y = _rmsnorm_f32(x_ref[...], g_ref[...])
        hb = y.astype(BF16)
        hs_ref[...] = hb
        h_ref[...] = hb
        y_hi = _trunc_bf16(y)
        hi = y_hi.astype(BF16)
        lo = (y - y_hi).astype(BF16)
        misc_ref[...] = (_dot(hi, wmh_ref[...]) + _dot(hi, wml_ref[...])) + _dot(lo, wmh_ref[...])

    proj_ref[...] = _dot(hs_ref[...], w_ref[...]).astype(proj_ref.dtype)


def norm_proj(x, g, w, wm_hi, wm_lo):
    T, D = x.shape
    N = w.shape[1]
    tm, tn = _tile(T, 512), _tile(N, 512)
    return pl.pallas_call(
        _norm_proj_kernel,
        grid=(T // tm, N // tn),
        in_specs=[
            pl.BlockSpec((tm, D), lambda i, j: (i, 0)),
            pl.BlockSpec((1, D), lambda i, j: (0, 0)),
            pl.BlockSpec((D, tn), lambda i, j: (0, j)),
            pl.BlockSpec((D, MISC_W), lambda i, j: (0, 0)),
            pl.BlockSpec((D, MISC_W), lambda i, j: (0, 0)),
        ],
        out_specs=[
            pl.BlockSpec((tm, tn), lambda i, j: (i, j)),
            pl.BlockSpec((tm, D), lambda i, j: (i, 0)),
            pl.BlockSpec((tm, MISC_W), lambda i, j: (i, 0)),
        ],
        out_shape=[
            jax.ShapeDtypeStruct((T, N), BF16),
            jax.ShapeDtypeStruct((T, D), BF16),
            jax.ShapeDtypeStruct((T, MISC_W), F32),
        ],
        scratch_shapes=[pltpu.VMEM((tm, D), BF16)],
        compiler_params=_cparams(("parallel", "arbitrary")),
        name="norm_proj",
    )(x, g.reshape(1, D), w, wm_hi, wm_lo)


def _merge_kernel(h_ref, o0_ref, o1_ref, o2_ref, wg_ref, wb_ref, out_ref):
    h = h_ref[...]
    acc = None
    for i, o_ref in enumerate((o0_ref, o1_ref, o2_ref)):
        y = jax.nn.sigmoid(_dot(h, wg_ref[i])) * _dot(o_ref[...], wb_ref[i])
        acc = y if acc is None else acc + y
    out_ref[...] = acc.astype(out_ref.dtype)


def merge_branches(h, o_sb, o_fox, o_dsa, wg, wb):
    T, D = h.shape
    W = o_sb.shape[1]
    N = wg.shape[2]
    tm, tn = _tile(T, 512), _tile(N, 512)
    o_spec = pl.BlockSpec((tm, W), lambda i, j: (i, 0))
    return pl.pallas_call(
        _merge_kernel,
        grid=(T // tm, N // tn),
        in_specs=[
            pl.BlockSpec((tm, D), lambda i, j: (i, 0)),
            o_spec, o_spec, o_spec,
            pl.BlockSpec((3, D, tn), lambda i, j: (0, 0, j)),
            pl.BlockSpec((3, W, tn), lambda i, j: (0, 0, j)),
        ],
        out_specs=pl.BlockSpec((tm, tn), lambda i, j: (i, j)),
        out_shape=jax.ShapeDtypeStruct((T, N), BF16),
        compiler_params=_cparams(("parallel", "arbitrary")),
        name="merge_branches",
    )(h, o_sb, o_fox, o_dsa, wg, wb)


def _matmul_residual_kernel(a_ref, b_ref, r_ref, o_ref):
    o_ref[...] = r_ref[...] + _dot(a_ref[...], b_ref[...])


def matmul_residual(a, b, r):
    T, K = a.shape
    N = b.shape[1]
    tm, tn = _tile(T, 512), _tile(N, 512)
    return pl.pallas_call(
        _matmul_residual_kernel,
        grid=(T // tm, N // tn),
        in_specs=[
            pl.BlockSpec((tm, K), lambda i, j: (i, 0)),
            pl.BlockSpec((K, tn), lambda i, j: (0, j)),
            pl.BlockSpec((tm, tn), lambda i, j: (i, j)),
        ],
        out_specs=pl.BlockSpec((tm, tn), lambda i, j: (i, j)),
        out_shape=jax.ShapeDtypeStruct((T, N), F32),
        compiler_params=_cparams(("parallel", "arbitrary")),
        name="matmul_residual",
    )(a, b, r)


def _ffn_up_kernel(x_ref, g_ref, wg_ref, wu_ref, u_ref, hs_ref):
    @pl.when(pl.program_id(1) == 0)
    def _():
        hs_ref[...] = _rmsnorm_f32(x_ref[...], g_ref[...]).astype(BF16)

    h = hs_ref[...]
    u_ref[...] = (jax.nn.silu(_dot(h, wg_ref[...])) * _dot(h, wu_ref[...])).astype(u_ref.dtype)


def ffn_up(x, g, wg, wu):
    T, D = x.shape
    Fd = wg.shape[1]
    tm, tn = _tile(T, 512), _tile(Fd, 512)
    return pl.pallas_call(
        _ffn_up_kernel,
        grid=(T // tm, Fd // tn),
        in_specs=[
            pl.BlockSpec((tm, D), lambda i, j: (i, 0)),
            pl.BlockSpec((1, D), lambda i, j: (0, 0)),
            pl.BlockSpec((D, tn), lambda i, j: (0, j)),
            pl.BlockSpec((D, tn), lambda i, j: (0, j)),
        ],
        out_specs=pl.BlockSpec((tm, tn), lambda i, j: (i, j)),
        out_shape=jax.ShapeDtypeStruct((T, Fd), BF16),
        scratch_shapes=[pltpu.VMEM((tm, D), BF16)],
        compiler_params=_cparams(("parallel", "arbitrary")),
        name="ffn_up",
    )(x, g.reshape(1, D), wg, wu)


def _online_softmax_update(s, v, m_ref, l_ref, acc_ref, idx):
    m_prev = m_ref[idx]
    m_new = jnp.maximum(m_prev, jnp.max(s, axis=1, keepdims=True))
    p = jnp.exp2(s - _rep(m_new, s.shape[1] // 128))
    alpha = jnp.exp2(m_prev - m_new)
    l_ref[idx] = alpha * l_ref[idx] + jnp.sum(p, axis=1, keepdims=True)
    acc_ref[idx] = alpha * acc_ref[idx] + _dot(p.astype(BF16), v)
    m_ref[idx] = m_new


def _sb_kernel(q_ref, k_ref, v_ref, o_ref, acc_ref, car_ref, *, t):
    i = pl.program_id(2)
    q = q_ref[0]
    row = lax.broadcasted_iota(jnp.int32, (t, t), 0)
    col = lax.broadcasted_iota(jnp.int32, (t, t), 1)
    upper = (row > col).astype(BF16)
    ones = jnp.ones((t, 128), BF16)
    strict = col < row

    def process(j, diag):
        start = pl.multiple_of(j * t, t)
        k = k_ref[0, pl.ds(start, t), :]
        v = v_ref[0, pl.ds(start, t), :]
        z = _dot_nt(q, k)
        log_keep = -(jnp.maximum(z, 0.0) + jnp.log(1.0 + jnp.exp(-jnp.abs(z))))
        lk = jnp.where(strict, log_keep, 0.0) if diag else log_keep
        hi_f = _trunc_bf16(lk)
        hi = hi_f.astype(BF16)
        lo = (lk - hi_f).astype(BF16)
        after = _dot(hi, upper) + _dot(lo, upper)
        total = _dot(hi, ones) + _dot(lo, ones)
        car = car_ref[...]
        a = jnp.exp(z + log_keep + after + _rep(car, t // 128))
        if diag:
            a = jnp.where(strict, a, 0.0)
        acc_ref[...] += _dot(a.astype(BF16), v)
        car = car + total
        car_ref[...] = car
        return jnp.max(car)

    acc_ref[...] = jnp.zeros_like(acc_ref)
    car_ref[...] = jnp.zeros_like(car_ref)
    mx = process(i, True)

    def cond(st):
        j, mx = st
        return jnp.logical_and(j >= 0, mx > -SB_SKIP)

    def body(st):
        j, _ = st
        return j - 1, process(j, False)

    lax.while_loop(cond, body, (i - 1, mx))
    o_ref[0] = acc_ref[...].astype(o_ref.dtype)


def sb_attention(proj3, q_col, k_col, v_col):
    B, S, _ = proj3.shape
    t = _tile(S, 256)
    return pl.pallas_call(
        functools.partial(_sb_kernel, t=t),
        grid=(B, N_HEADS, S // t),
        in_specs=[
            pl.BlockSpec((1, t, HEAD_DIM), lambda b, h, i: (b, i, q_col + h)),
            pl.BlockSpec((1, S, HEAD_DIM), lambda b, h, i: (b, 0, k_col + h)),
            pl.BlockSpec((1, S, HEAD_DIM), lambda b, h, i: (b, 0, v_col + h)),
        ],
        out_specs=pl.BlockSpec((1, t, HEAD_DIM), lambda b, h, i: (b, i, h)),
        out_shape=jax.ShapeDtypeStruct((B, S, BRANCH_W), BF16),
        scratch_shapes=[pltpu.VMEM((t, HEAD_DIM), F32), pltpu.VMEM((t, HEAD_DIM), F32)],
        compiler_params=_cparams(("parallel", "parallel", "arbitrary")),
        name="sb_attention",
    )(proj3, proj3, proj3)


def _fox_kernel(q_ref, k_ref, v_ref, o_ref, m_ref, l_ref, acc_ref, *, tq, tk):
    i = pl.program_id(2)
    q = q_ref[0]
    m_ref[...] = jnp.full_like(m_ref, NEG)
    l_ref[...] = jnp.zeros_like(l_ref)
    acc_ref[...] = jnp.zeros_like(acc_ref)
    row = lax.broadcasted_iota(jnp.int32, (tq, tk), 0)
    col = lax.broadcasted_iota(jnp.int32, (tq, tk), 1)

    def step(start, mask):
        start = pl.multiple_of(start, tk)
        s = _dot_nt(q, k_ref[0, pl.ds(start, tk), :])
        if mask is not None:
            s = jnp.where(mask, s, NEG)
        _online_softmax_update(s, v_ref[0, pl.ds(start, tk), :], m_ref, l_ref, acc_ref, 0)

    def body(j, c):
        step(j * tk, None)
        return c

    lax.fori_loop(0, i * (tq // tk), body, 0)
    for d in range(tq // tk):
        step(i * tq + d * tk, row >= col + d * tk)
    o_ref[0] = (acc_ref[0] / l_ref[0]).astype(o_ref.dtype)


def fox_attention(q_ext, k_ext, proj3, v_col):
    B, S, _ = q_ext.shape
    tq = _tile(S, FOX_TQ)
    tk = _tile(tq, FOX_TK)
    E = 2 * HEAD_DIM
    return pl.pallas_call(
        functools.partial(_fox_kernel, tq=tq, tk=tk),
        grid=(B, N_HEADS, S // tq),
        in_specs=[
            pl.BlockSpec((1, tq, E), lambda b, h, i: (b, i, h)),
            pl.BlockSpec((1, S, E), lambda b, h, i: (b, 0, h)),
            pl.BlockSpec((1, S, HEAD_DIM), lambda b, h, i: (b, 0, v_col + h)),
        ],
        out_specs=pl.BlockSpec((1, tq, HEAD_DIM), lambda b, h, i: (b, i, h)),
        out_shape=jax.ShapeDtypeStruct((B, S, BRANCH_W), BF16),
        scratch_shapes=[pltpu.VMEM((1, tq, HEAD_DIM), F32)] * 3,
        compiler_params=_cparams(("parallel", "parallel", "arbitrary")),
        name="fox_attention",
    )(q_ext, k_ext, proj3)


def _dsa_kernel(iqs_ref, w_ref, ik_ref, qc_ref, kc_ref, vc_ref, o_ref,
                key_ref, k16_ref, wrep_ref, m_ref, l_ref, acc_ref, *, tq, tk, topk):
    i = pl.program_id(1)
    nkb = (i * tq + tq + tk - 1) // tk
    rep = N_HEADS // N_KV_DSA
    nrep = tk // 128
    kf = float(topk)

    w = w_ref[0]
    for h in range(IDX_HEADS):
        wrep_ref[h] = jnp.broadcast_to(w[:, h:h + 1], (tq, 128))
    rowpos = i * tq + lax.broadcasted_iota(jnp.int32, (tq, tk), 0)
    colio = lax.broadcasted_iota(jnp.int32, (tq, tk), 1)

    def score_block(jk, c):
        start = pl.multiple_of(jk * tk, tk)
        ikb = ik_ref[0, pl.ds(start, tk), :]
        score = jnp.zeros((tq, tk), F32)
        for hg in range(IDX_HEADS // IDX_GROUP):
            z = _dot_nt(iqs_ref[0, 0, hg * IDX_GROUP * tq:(hg + 1) * IDX_GROUP * tq, :], ikb)
            for r in range(IDX_GROUP):
                score = score + (jnp.maximum(z[r * tq:(r + 1) * tq], 0.0)
                                 * _rep(wrep_ref[hg * IDX_GROUP + r], nrep))
        bits = lax.bitcast_convert_type(score, jnp.int32)
        key = bits ^ ((bits >> 31) & jnp.int32(0x7FFFFFFF))
        key = jnp.where(start + colio <= rowpos, key, jnp.int32(INT_MIN))
        key_ref[jk] = key
        k16_ref[jk] = (key >> 16).astype(jnp.int16)
        return c

    lax.fori_loop(0, nkb, score_block, 0)

    def count_ge(cand):
        cw = _rep(cand.astype(jnp.int16), nrep)

        def blk(jk, c):
            hit = jnp.where(k16_ref[jk] >= cw, jnp.int16(1), jnp.int16(0))
            for t in range(nrep):
                c = c + hit[:, t * 128:(t + 1) * 128]
            return c

        c = lax.fori_loop(0, nkb, blk, jnp.zeros((tq, 128), jnp.int16))
        cnt = jnp.sum(c.astype(jnp.int32).astype(F32), axis=1, keepdims=True)
        return jnp.broadcast_to(cnt, (tq, 128))

    def bisect16(need, count_all):
        def bit(it, st):
            tau, cge = st
            cand = tau + lax.shift_left(jnp.int32(1), 15 - it)
            cnt = count_ge(cand)
            ok = cnt >= need
            return jnp.where(ok, cand, tau), jnp.where(ok, cnt, cge)

        return lax.fori_loop(0, 16, bit, (jnp.full((tq, 128), -32768, jnp.int32), count_all))

    n_all = jnp.zeros((tq, 128), F32) + (nkb * tk).astype(F32)
    tau_hi, cge_hi = bisect16(kf, n_all)
    n_gt = count_ge(tau_hi + 1)
    t16w = _rep(tau_hi.astype(jnp.int16), nrep)

    def low_block(jk, c):
        lo = ((key_ref[jk] & jnp.int32(0xFFFF)) - 32768).astype(jnp.int16)
        k16_ref[jk] = jnp.where(k16_ref[jk] == t16w, lo, jnp.int16(-32768))
        return c

    lax.fori_loop(0, nkb, low_block, 0)
    tau_lo, _ = bisect16(kf - n_gt, cge_hi - n_gt)
    tau = jnp.maximum(lax.shift_left(tau_hi, 16) | (tau_lo + 32768), jnp.int32(INT_MIN + 1))
    tauw = _rep(tau, nrep)

    m_ref[...] = jnp.full_like(m_ref, NEG)
    l_ref[...] = jnp.zeros_like(l_ref)
    acc_ref[...] = jnp.zeros_like(acc_ref)
    qc = qc_ref[0]
    qg = [jnp.concatenate([qc[:, (rep * g + r) * HEAD_DIM:(rep * g + r + 1) * HEAD_DIM]
                           for r in range(rep)], axis=0) for g in range(N_KV_DSA)]

    def attend_block(jk, c):
        start = pl.multiple_of(jk * tk, tk)
        bias = jnp.where(key_ref[jk] >= tauw, 0.0, NEG)
        bias = jnp.concatenate([bias] * rep, axis=0)
        for g in range(N_KV_DSA):
            kg = kc_ref[0, pl.ds(start, tk), g * HEAD_DIM:(g + 1) * HEAD_DIM]
            vg = vc_ref[0, pl.ds(start, tk), g * HEAD_DIM:(g + 1) * HEAD_DIM]
            _online_softmax_update(_dot_nt(qg[g], kg) + bias, vg, m_ref, l_ref, acc_ref, g)
        return c

    lax.fori_loop(0, nkb, attend_block, 0)
    for g in range(N_KV_DSA):
        og = (acc_ref[g] / l_ref[g]).astype(o_ref.dtype)
        for r in range(rep):
            hcol = (rep * g + r) * HEAD_DIM
            o_ref[0, :, hcol:hcol + HEAD_DIM] = og[r * tq:(r + 1) * tq]


def dsa_attention(iqs, iw, ik, qc, kc, proj3, v_col, topk):
    B, S, _ = qc.shape
    tq = _tile(S, DSA_TQ)
    tk = _tile(S, DSA_TK)
    kvw = N_KV_DSA * HEAD_DIM
    rows = (N_HEADS // N_KV_DSA) * tq
    resident = pl.Buffered(1)
    return pl.pallas_call(
        functools.partial(_dsa_kernel, tq=tq, tk=tk, topk=topk),
        grid=(B, S // tq),
        in_specs=[
            pl.BlockSpec((1, 1, IDX_HEADS * tq, IDX_DIM), lambda b, i: (b, i, 0, 0)),
            pl.BlockSpec((1, tq, IDX_HEADS), lambda b, i: (b, i, 0)),
            pl.BlockSpec((1, S, IDX_DIM), lambda b, i: (b, 0, 0), pipeline_mode=resident),
            pl.BlockSpec((1, tq, BRANCH_W), lambda b, i: (b, i, 0)),
            pl.BlockSpec((1, S, kvw), lambda b, i: (b, 0, 0), pipeline_mode=resident),
            pl.BlockSpec((1, S, kvw), lambda b, i: (b, 0, v_col), pipeline_mode=resident),
        ],
        out_specs=pl.BlockSpec((1, tq, BRANCH_W), lambda b, i: (b, i, 0)),
        out_shape=jax.ShapeDtypeStruct((B, S, BRANCH_W), BF16),
        scratch_shapes=[
            pltpu.VMEM((S // tk, tq, tk), jnp.int32),
            pltpu.VMEM((S // tk, tq, tk), jnp.int16),
            pltpu.VMEM((IDX_HEADS, tq, 128), F32),
            pltpu.VMEM((N_KV_DSA, rows, HEAD_DIM), F32),
            pltpu.VMEM((N_KV_DSA, rows, HEAD_DIM), F32),
            pltpu.VMEM((N_KV_DSA, rows, HEAD_DIM), F32),
        ],
        compiler_params=_cparams(("parallel", "arbitrary")),
        name="dsa_attention",
    )(iqs, iw, ik, qc, kc, proj3)


def _rope_tables(seq, dim):
    inv = 1.0 / (ROPE_THETA ** (jnp.arange(0, dim, 2, dtype=F32) / dim))
    ang = jnp.arange(seq, dtype=F32)[:, None] * inv[None, :]
    return jnp.cos(ang), jnp.sin(ang)


def _rope(x, cos, sin):
    x1, x2 = jnp.split(x, 2, axis=-1)
    c, s = cos[None, :, None, :], sin[None, :, None, :]
    return jnp.concatenate([x1 * c - x2 * s, x2 * c + x1 * s], axis=-1)


def _head_norm(x, g):
    return x * lax.rsqrt(jnp.mean(x * x, axis=-1, keepdims=True) + NORM_EPS) * g


def _split3(c):
    c1 = _trunc_bf16(c)
    r = c - c1
    c2 = _trunc_bf16(r)
    return c1.astype(BF16), c2.astype(BF16), (r - c2).astype(BF16)


def _pack_in_weights(w_in):
    D = w_in.shape[0]
    scale = HEAD_DIM ** -0.5
    o = 0
    sb = w_in[:, o:o + 3 * BRANCH_W]; o += 3 * BRANCH_W
    fox = w_in[:, o:o + 3 * BRANCH_W]; o += 3 * BRANCH_W
    fox_f = w_in[:, o:o + N_HEADS]; o += N_HEADS
    dsa_w = BRANCH_W + 2 * N_KV_DSA * HEAD_DIM
    dsa = w_in[:, o:o + dsa_w]; o += dsa_w
    idx_q = w_in[:, o:o + IDX_HEADS * IDX_DIM]; o += IDX_HEADS * IDX_DIM
    idx_k = w_in[:, o:o + IDX_DIM]; o += IDX_DIM
    idx_w = w_in[:, o:o + IDX_HEADS]; o += IDX_HEADS
    sb = jnp.concatenate([sb[:, :BRANCH_W] * scale, sb[:, BRANCH_W:]], axis=1)
    w_main = jnp.concatenate([sb, fox, dsa, idx_q], axis=1).astype(BF16)
    pad = jnp.zeros((D, MISC_W - IDX_DIM - IDX_HEADS - N_HEADS), F32)
    misc = jnp.concatenate([idx_k, idx_w, fox_f, pad], axis=1)
    m_hi = _trunc_bf16(misc)
    return w_main, m_hi.astype(BF16), (misc - m_hi).astype(BF16)


def _layer(x2, B, S, rope_h, rope_i, topk, norm_mix_g, w_in, fox_f_bias, fox_q_g, fox_k_g,
           dsa_q_g, dsa_k_g, w_gate, w_branch, w_out, norm_ffn_g, w_ffn_gate, w_ffn_up, w_ffn_down):
    T, D = x2.shape
    scale = HEAD_DIM ** -0.5
    w_main, wm_hi, wm_lo = _pack_in_weights(w_in)
    proj, h, misc = norm_proj(x2, norm_mix_g, w_main, wm_hi, wm_lo)
    proj3 = proj.reshape(B, S, PROJ_W)
    misc = misc.reshape(B, S, MISC_W)
    nblk = BRANCH_W // HEAD_DIM

    o_sb = sb_attention(proj3, 0, nblk, 2 * nblk)

    fq = proj3[..., 3 * BRANCH_W:4 * BRANCH_W].astype(F32).reshape(B, S, N_HEADS, HEAD_DIM)
    fk = proj3[..., 4 * BRANCH_W:5 * BRANCH_W].astype(F32).reshape(B, S, N_HEADS, HEAD_DIM)
    fq = _head_norm(fq, fox_q_g * (scale * LOG2E)).astype(BF16)
    fk = _head_norm(fk, fox_k_g).astype(BF16)
    fox_f = misc[..., IDX_DIM + IDX_HEADS:IDX_DIM + IDX_HEADS + N_HEADS]
    c = jnp.cumsum(jax.nn.log_sigmoid(fox_f + fox_f_bias), axis=1) * LOG2E
    c1, c2, c3 = _split3(c)
    one = jnp.ones_like(c1)
    zpad = jnp.zeros((B, S, N_HEADS, HEAD_DIM - 6), BF16)
    st = lambda parts: jnp.stack(parts, axis=-1)
    q_ext = jnp.concatenate([fq, st([c1, c2, c3, one, one, one]), zpad], axis=-1)
    k_ext = jnp.concatenate([fk, st([one, one, one, -c1, -c2, -c3]), zpad], axis=-1)
    o_fox = fox_attention(q_ext.reshape(B, S, -1), k_ext.reshape(B, S, -1), proj3, 5 * nblk)

    cos_h, sin_h = rope_h
    cos_i, sin_i = rope_i
    o = 6 * BRANCH_W
    dq = proj3[..., o:o + BRANCH_W].astype(F32).reshape(B, S, N_HEADS, HEAD_DIM)
    dk = proj3[..., o + BRANCH_W:o + BRANCH_W + N_KV_DSA * HEAD_DIM].astype(F32)
    dk = dk.reshape(B, S, N_KV_DSA, HEAD_DIM)
    qc = _rope(_head_norm(dq, dsa_q_g * (scale * LOG2E)), cos_h, sin_h).astype(BF16).reshape(B, S, BRANCH_W)
    kc = _rope(_head_norm(dk, dsa_k_g), cos_h, sin_h).astype(BF16).reshape(B, S, -1)
    o_iq = o + BRANCH_W + 2 * N_KV_DSA * HEAD_DIM
    iq = proj3[..., o_iq:o_iq + IDX_HEADS * IDX_DIM].astype(F32).reshape(B, S, IDX_HEADS, IDX_DIM)
    iq = _rope(iq, cos_i, sin_i).astype(BF16)
    tq = _tile(S, DSA_TQ)
    iqs = iq.reshape(B, S // tq, tq, IDX_HEADS, IDX_DIM).transpose(0, 1, 3, 2, 4)
    iqs = iqs.reshape(B, S // tq, IDX_HEADS * tq, IDX_DIM)
    ik = _rope(misc[..., None, :IDX_DIM], cos_i, sin_i)[:, :, 0, :].astype(BF16)
    iw = misc[..., IDX_DIM:IDX_DIM + IDX_HEADS] * (IDX_HEADS ** -0.5 * IDX_DIM ** -0.5)
    v_col = (o + BRANCH_W + N_KV_DSA * HEAD_DIM) // (N_KV_DSA * HEAD_DIM)
    o_dsa = dsa_attention(iqs, iw, ik, qc, kc, proj3, v_col, topk)

    merged = merge_branches(h, o_sb.reshape(T, -1), o_fox.reshape(T, -1), o_dsa.reshape(T, -1),
                            w_gate.astype(BF16), w_branch.astype(BF16))
    x2 = matmul_residual(merged, w_out.astype(BF16), x2)
    u = ffn_up(x2, norm_ffn_g, w_ffn_gate.astype(BF16), w_ffn_up.astype(BF16))
    return matmul_residual(u, w_ffn_down.astype(BF16), x2)


def kernel(x, norm_mix_g, w_in, fox_f_bias, fox_q_g, fox_k_g, dsa_q_g, dsa_k_g, w_gate, w_branch,
           w_out, norm_ffn_g, w_ffn_gate, w_ffn_up, w_ffn_down):
    B, S, D = x.shape
    topk = min(TOPK_MAX, S // 4)
    rope_h = _rope_tables(S, HEAD_DIM)
    rope_i = _rope_tables(S, IDX_DIM)
    x2 = x.reshape(B * S, D)
    for l in range(norm_mix_g.shape[0]):
        x2 = _layer(x2, B, S, rope_h, rope_i, topk, norm_mix_g[l], w_in[l], fox_f_bias[l],
                    fox_q_g[l], fox_k_g[l], dsa_q_g[l], dsa_k_g[l], w_gate[l], w_branch[l],
                    w_out[l], norm_ffn_g[l], w_ffn_gate[l], w_ffn_up[l], w_ffn_down[l])
    return x2.reshape(B, S, D)
```

```python
import functools

import jax
import jax.numpy as jnp
import numpy as np
from jax import lax
from jax.experimental import pallas as pl
from jax.experimental.pallas import tpu as pltpu

F32 = jnp.float32
BF16 = jnp.bfloat16

HEAD_DIM = 128
N_HEADS = 8
N_KV_DSA = 2
IDX_HEADS = 16
IDX_DIM = 64
TOPK_MAX = 256
ROPE_THETA = 10000.0
NORM_EPS = 1e-6
BRANCH_W = N_HEADS * HEAD_DIM
MISC_W = 128
FOXF_LANE = IDX_DIM + IDX_HEADS
COL_SB = 0
COL_FOX = 3 * BRANCH_W
COL_DSA_Q = 6 * BRANCH_W
COL_IDX_Q = 7 * BRANCH_W
COL_DSA_K = 8 * BRANCH_W
COL_DSA_V = COL_DSA_K + N_KV_DSA * HEAD_DIM
PROJ_W = COL_DSA_V + N_KV_DSA * HEAD_DIM
PREP_TM = 256

LOG2E = 1.4426950408889634
FOX_TQ, FOX_TK = 1024, 512
DSA_TQ, DSA_TK = 256, 512
IDX_GROUP = 4
NEG = -1e30
INT_MIN = -(2 ** 31)
SB_SKIP = 100.0
VMEM_LIMIT_BYTES = 56 * 1024 * 1024


def _cparams(sem):
    return pltpu.CompilerParams(dimension_semantics=sem, vmem_limit_bytes=VMEM_LIMIT_BYTES)


def _tile(n, pref):
    t = pref
    while n % t:
        t //= 2
    return t


def _dot(a, b):
    return jnp.dot(a, b, preferred_element_type=F32)


def _dot_nt(a, b):
    return lax.dot_general(a, b, (((1,), (1,)), ((), ())), preferred_element_type=F32)


def _rep(x, n):
    return x if n == 1 else jnp.concatenate([x] * n, axis=1)


def _rmsnorm_f32(x, g):
    return x * lax.rsqrt(jnp.mean(x * x, axis=-1, keepdims=True) + NORM_EPS) * g


def _trunc_bf16(x):
    bits = lax.bitcast_convert_type(x, jnp.uint32) & jnp.uint32(0xFFFF0000)
    return lax.bitcast_convert_type(bits, F32)


def _norm_proj_kernel(x_ref, g_ref, w_ref, wmh_ref, wml_ref, proj_ref, h_ref, misc_ref, hs_ref):
    @pl.when(pl.program_id(1) == 0)
    def _():
        y = _rmsnorm_f32(x_ref[...], g_ref[...])
        hb = y.astype(BF16)
        hs_ref[...] = hb
        h_ref[...] = hb
        y_hi = _trunc_bf16(y)
        hi = y_hi.astype(BF16)
        lo = (y - y_hi).astype(BF16)
        misc_ref[...] = (_dot(hi, wmh_ref[...]) + _dot(hi, wml_ref[...])) + _dot(lo, wmh_ref[...])

    proj_ref[...] = _dot(hs_ref[...], w_ref[...]).astype(proj_ref.dtype)


def norm_proj(x, g, w, wm_hi, wm_lo):
    T, D = x.shape
    N = w.shape[1]
    tm, tn = _tile(T, 512), _tile(N, 512)
    return pl.pallas_call(
        _norm_proj_kernel,
        grid=(T // tm, N // tn),
        in_specs=[
            pl.BlockSpec((tm, D), lambda i, j: (i, 0)),
            pl.BlockSpec((1, D), lambda i, j: (0, 0)),
            pl.BlockSpec((D, tn), lambda i, j: (0, j)),
            pl.BlockSpec((D, MISC_W), lambda i, j: (0, 0)),
            pl.BlockSpec((D, MISC_W), lambda i, j: (0, 0)),
        ],
        out_specs=[
            pl.BlockSpec((tm, tn), lambda i, j: (i, j)),
            pl.BlockSpec((tm, D), lambda i, j: (i, 0)),
            pl.BlockSpec((tm, MISC_W), lambda i, j: (i, 0)),
        ],
        out_shape=[
            jax.ShapeDtypeStruct((T, N), BF16),
            jax.ShapeDtypeStruct((T, D), BF16),
            jax.ShapeDtypeStruct((T, MISC_W), F32),
        ],
        scratch_shapes=[pltpu.VMEM((tm, D), BF16)],
        compiler_params=_cparams(("parallel", "arbitrary")),
        name="norm_proj",
    )(x, g.reshape(1, D), w, wm_hi, wm_lo)


def _merge_kernel(h_ref, o0_ref, o1_ref, o2_ref, wg_ref, wb_ref, out_ref):
    h = h_ref[...]
    acc = None
    for i, o_ref in enumerate((o0_ref, o1_ref, o2_ref)):
        y = jax.nn.sigmoid(_dot(h, wg_ref[i])) * _dot(o_ref[...], wb_ref[i])
        acc = y if acc is None else acc + y
    out_ref[...] = acc.astype(out_ref.dtype)


def merge_branches(h, o_sb, o_fox, o_dsa, wg, wb):
    T, D = h.shape
    W = o_sb.shape[1]
    N = wg.shape[2]
    tm, tn = _tile(T, 512), _tile(N, 512)
    o_spec = pl.BlockSpec((tm, W), lambda i, j: (i, 0))
    return pl.pallas_call(
        _merge_kernel,
        grid=(T // tm, N // tn),
        in_specs=[
            pl.BlockSpec((tm, D), lambda i, j: (i, 0)),
            o_spec, o_spec, o_spec,
            pl.BlockSpec((3, D, tn), lambda i, j: (0, 0, j)),
            pl.BlockSpec((3, W, tn), lambda i, j: (0, 0, j)),
        ],
        out_specs=pl.BlockSpec((tm, tn), lambda i, j: (i, j)),
        out_shape=jax.ShapeDtypeStruct((T, N), BF16),
        compiler_params=_cparams(("parallel", "arbitrary")),
        name="merge_branches",
    )(h, o_sb, o_fox, o_dsa, wg, wb)


def _matmul_residual_kernel(a_ref, b_ref, r_ref, o_ref):
    o_ref[...] = r_ref[...] + _dot(a_ref[...], b_ref[...])


def matmul_residual(a, b, r):
    T, K = a.shape
    N = b.shape[1]
    tm, tn = _tile(T, 512), _tile(N, 512)
    return pl.pallas_call(
        _matmul_residual_kernel,
        grid=(T // tm, N // tn),
        in_specs=[
            pl.BlockSpec((tm, K), lambda i, j: (i, 0)),
            pl.BlockSpec((K, tn), lambda i, j: (0, j)),
            pl.BlockSpec((tm, tn), lambda i, j: (i, j)),
        ],
        out_specs=pl.BlockSpec((tm, tn), lambda i, j: (i, j)),
        out_shape=jax.ShapeDtypeStruct((T, N), F32),
        compiler_params=_cparams(("parallel", "arbitrary")),
        name="matmul_residual",
    )(a, b, r)


def _ffn_up_kernel(x_ref, g_ref, wg_ref, wu_ref, u_ref, hs_ref):
    @pl.when(pl.program_id(1) == 0)
    def _():
        hs_ref[...] = _rmsnorm_f32(x_ref[...], g_ref[...]).astype(BF16)

    h = hs_ref[...]
    u_ref[...] = (jax.nn.silu(_dot(h, wg_ref[...])) * _dot(h, wu_ref[...])).astype(u_ref.dtype)


def ffn_up(x, g, wg, wu):
    T, D = x.shape
    Fd = wg.shape[1]
    tm, tn = _tile(T, 512), _tile(Fd, 512)
    return pl.pallas_call(
        _ffn_up_kernel,
        grid=(T // tm, Fd // tn),
        in_specs=[
            pl.BlockSpec((tm, D), lambda i, j: (i, 0)),
            pl.BlockSpec((1, D), lambda i, j: (0, 0)),
            pl.BlockSpec((D, tn), lambda i, j: (0, j)),
            pl.BlockSpec((D, tn), lambda i, j: (0, j)),
        ],
        out_specs=pl.BlockSpec((tm, tn), lambda i, j: (i, j)),
        out_shape=jax.ShapeDtypeStruct((T, Fd), BF16),
        scratch_shapes=[pltpu.VMEM((tm, D), BF16)],
        compiler_params=_cparams(("parallel", "arbitrary")),
        name="ffn_up",
    )(x, g.reshape(1, D), wg, wu)


def _online_softmax_update(s, v, m_ref, l_ref, acc_ref, idx):
    m_prev = m_ref[idx]
    m_new = jnp.maximum(m_prev, jnp.max(s, axis=1, keepdims=True))
    p = jnp.exp2(s - _rep(m_new, s.shape[1] // 128))
    alpha = jnp.exp2(m_prev - m_new)
    l_ref[idx] = alpha * l_ref[idx] + jnp.sum(p, axis=1, keepdims=True)
    acc_ref[idx] = alpha * acc_ref[idx] + _dot(p.astype(BF16), v)
    m_ref[idx] = m_new


def _sb_kernel(q_ref, k_ref, v_ref, o_ref, acc_ref, car_ref, *, t):
    i = pl.program_id(2)
    q = q_ref[0]
    row = lax.broadcasted_iota(jnp.int32, (t, t), 0)
    col = lax.broadcasted_iota(jnp.int32, (t, t), 1)
    upper = (row > col).astype(BF16)
    ones = jnp.ones((t, 128), BF16)
    strict = col < row

    def process(j, diag):
        start = pl.multiple_of(j * t, t)
        k = k_ref[0, pl.ds(start, t), :]
        v = v_ref[0, pl.ds(start, t), :]
        z = _dot_nt(q, k)
        log_keep = -(jnp.maximum(z, 0.0) + jnp.log(1.0 + jnp.exp(-jnp.abs(z))))
        lk = jnp.where(strict, log_keep, 0.0) if diag else log_keep
        hi_f = _trunc_bf16(lk)
        hi = hi_f.astype(BF16)
        lo = (lk - hi_f).astype(BF16)
        after = _dot(hi, upper) + _dot(lo, upper)
        total = _dot(hi, ones) + _dot(lo, ones)
        car = car_ref[...]
        a = jnp.exp(z + log_keep + after + _rep(car, t // 128))
        if diag:
            a = jnp.where(strict, a, 0.0)
        acc_ref[...] += _dot(a.astype(BF16), v)
        car = car + total
        car_ref[...] = car
        return jnp.max(car)

    acc_ref[...] = jnp.zeros_like(acc_ref)
    car_ref[...] = jnp.zeros_like(car_ref)
    mx = process(i, True)

    def cond(st):
        j, mx = st
        return jnp.logical_and(j >= 0, mx > -SB_SKIP)

    def body(st):
        j, _ = st
        return j - 1, process(j, False)

    lax.while_loop(cond, body, (i - 1, mx))
    o_ref[0] = acc_ref[...].astype(o_ref.dtype)


def sb_attention(proj3, q_col, k_col, v_col):
    B, S, _ = proj3.shape
    t = _tile(S, 256)
    return pl.pallas_call(
        functools.partial(_sb_kernel, t=t),
        grid=(B, N_HEADS, S // t),
        in_specs=[
            pl.BlockSpec((1, t, HEAD_DIM), lambda b, h, i: (b, i, q_col + h)),
            pl.BlockSpec((1, S, HEAD_DIM), lambda b, h, i: (b, 0, k_col + h)),
            pl.BlockSpec((1, S, HEAD_DIM), lambda b, h, i: (b, 0, v_col + h)),
        ],
        out_specs=pl.BlockSpec((1, t, HEAD_DIM), lambda b, h, i: (b, i, h)),
        out_shape=jax.ShapeDtypeStruct((B, S, BRANCH_W), BF16),
        scratch_shapes=[pltpu.VMEM((t, HEAD_DIM), F32), pltpu.VMEM((t, HEAD_DIM), F32)],
        compiler_params=_cparams(("parallel", "parallel", "arbitrary")),
        name="sb_attention",
    )(proj3, proj3, proj3)


def _fox_kernel(q_ref, k_ref, v_ref, o_ref, m_ref, l_ref, acc_ref, *, tq, tk):
    i = pl.program_id(2)
    q = q_ref[0]
    m_ref[...] = jnp.full_like(m_ref, NEG)
    l_ref[...] = jnp.zeros_like(l_ref)
    acc_ref[...] = jnp.zeros_like(acc_ref)
    row = lax.broadcasted_iota(jnp.int32, (tq, tk), 0)
    col = lax.broadcasted_iota(jnp.int32, (tq, tk), 1)

    def step(start, mask):
        start = pl.multiple_of(start, tk)
        s = _dot_nt(q, k_ref[0, pl.ds(start, tk), :])
        if mask is not None:
            s = jnp.where(mask, s, NEG)
        _online_softmax_update(s, v_ref[0, pl.ds(start, tk), :], m_ref, l_ref, acc_ref, 0)

    def body(j, c):
        step(j * tk, None)
        return c

    lax.fori_loop(0, i * (tq // tk), body, 0)
    for d in range(tq // tk):
        step(i * tq + d * tk, row >= col + d * tk)
    o_ref[0] = (acc_ref[0] / l_ref[0]).astype(o_ref.dtype)


def fox_attention(q_ext, k_ext, proj3, v_col):
    B, S, _ = q_ext.shape
    tq = _tile(S, FOX_TQ)
    tk = _tile(tq, FOX_TK)
    E = 2 * HEAD_DIM
    return pl.pallas_call(
        functools.partial(_fox_kernel, tq=tq, tk=tk),
        grid=(B, N_HEADS, S // tq),
        in_specs=[
            pl.BlockSpec((1, tq, E), lambda b, h, i: (b, i, h)),
            pl.BlockSpec((1, S, E), lambda b, h, i: (b, 0, h)),
            pl.BlockSpec((1, S, HEAD_DIM), lambda b, h, i: (b, 0, v_col + h)),
        ],
        out_specs=pl.BlockSpec((1, tq, HEAD_DIM), lambda b, h, i: (b, i, h)),
        out_shape=jax.ShapeDtypeStruct((B, S, BRANCH_W), BF16),
        scratch_shapes=[pltpu.VMEM((1, tq, HEAD_DIM), F32)] * 3,
        compiler_params=_cparams(("parallel", "parallel", "arbitrary")),
        name="fox_attention",
    )(q_ext, k_ext, proj3)


def _dsa_kernel(iq_ref, w_ref, ik_ref, qc_ref, kc_ref, vc_ref, o_ref,
                iqs_ref, key_ref, k16_ref, wrep_ref, m_ref, l_ref, acc_ref, *, tq, tk, topk):
    i = pl.program_id(1)
    nkb = (i * tq + tq + tk - 1) // tk
    rep = N_HEADS // N_KV_DSA
    nrep = tk // 128
    kf = float(topk)

    w = w_ref[0]
    for h in range(IDX_HEADS):
        wrep_ref[h] = jnp.broadcast_to(w[:, h:h + 1], (tq, 128))
        iqs_ref[h * tq:(h + 1) * tq, :] = iq_ref[0, :, h * IDX_DIM:(h + 1) * IDX_DIM]
    rowpos = i * tq + lax.broadcasted_iota(jnp.int32, (tq, tk), 0)
    colio = lax.broadcasted_iota(jnp.int32, (tq, tk), 1)

    def score_block(jk, c):
        start = pl.multiple_of(jk * tk, tk)
        ikb = ik_ref[0, pl.ds(start, tk), :]
        score = jnp.zeros((tq, tk), F32)
        for hg in range(IDX_HEADS // IDX_GROUP):
            z = _dot_nt(iqs_ref[hg * IDX_GROUP * tq:(hg + 1) * IDX_GROUP * tq, :], ikb)
            for r in range(IDX_GROUP):
                score = score + (jnp.maximum(z[r * tq:(r + 1) * tq], 0.0)
                                 * _rep(wrep_ref[hg * IDX_GROUP + r], nrep))
        bits = lax.bitcast_convert_type(score, jnp.int32)
        key = bits ^ ((bits >> 31) & jnp.int32(0x7FFFFFFF))
        key = jnp.where(start + colio <= rowpos, key, jnp.int32(INT_MIN))
        key_ref[jk] = key
        k16_ref[jk] = (key >> 16).astype(jnp.int16)
        return c

    lax.fori_loop(0, nkb, score_block, 0)

    def count_ge(cand):
        cw = _rep(cand.astype(jnp.int16), nrep)

        def blk(jk, c):
            hit = jnp.where(k16_ref[jk] >= cw, jnp.int16(1), jnp.int16(0))
            for t in range(nrep):
                c = c + hit[:, t * 128:(t + 1) * 128]
            return c

        c = lax.fori_loop(0, nkb, blk, jnp.zeros((tq, 128), jnp.int16))
        cnt = jnp.sum(c.astype(jnp.int32).astype(F32), axis=1, keepdims=True)
        return jnp.broadcast_to(cnt, (tq, 128))

    def bisect16(need, count_all):
        def bit(it, st):
            tau, cge = st
            cand = tau + lax.shift_left(jnp.int32(1), 15 - it)
            cnt = count_ge(cand)
            ok = cnt >= need
            return jnp.where(ok, cand, tau), jnp.where(ok, cnt, cge)

        return lax.fori_loop(0, 16, bit, (jnp.full((tq, 128), -32768, jnp.int32), count_all))

    n_all = jnp.zeros((tq, 128), F32) + (nkb * tk).astype(F32)
    tau_hi, cge_hi = bisect16(kf, n_all)
    n_gt = count_ge(tau_hi + 1)
    t16w = _rep(tau_hi.astype(jnp.int16), nrep)

    def low_block(jk, c):
        lo = ((key_ref[jk] & jnp.int32(0xFFFF)) - 32768).astype(jnp.int16)
        k16_ref[jk] = jnp.where(k16_ref[jk] == t16w, lo, jnp.int16(-32768))
        return c

    lax.fori_loop(0, nkb, low_block, 0)
    tau_lo, _ = bisect16(kf - n_gt, cge_hi - n_gt)
    tau = jnp.maximum(lax.shift_left(tau_hi, 16) | (tau_lo + 32768), jnp.int32(INT_MIN + 1))
    tauw = _rep(tau, nrep)

    m_ref[...] = jnp.full_like(m_ref, NEG)
    l_ref[...] = jnp.zeros_like(l_ref)
    acc_ref[...] = jnp.zeros_like(acc_ref)
    qc = qc_ref[0]
    qg = [jnp.concatenate([qc[:, (rep * g + r) * HEAD_DIM:(rep * g + r + 1) * HEAD_DIM]
                           for r in range(rep)], axis=0) for g in range(N_KV_DSA)]

    def attend_block(jk, c):
        start = pl.multiple_of(jk * tk, tk)
        bias = jnp.where(key_ref[jk] >= tauw, 0.0, NEG)
        bias = jnp.concatenate([bias] * rep, axis=0)
        for g in range(N_KV_DSA):
            kg = kc_ref[0, pl.ds(start, tk), g * HEAD_DIM:(g + 1) * HEAD_DIM]
            vg = vc_ref[0, pl.ds(start, tk), g * HEAD_DIM:(g + 1) * HEAD_DIM]
            _online_softmax_update(_dot_nt(qg[g], kg) + bias, vg, m_ref, l_ref, acc_ref, g)
        return c

    lax.fori_loop(0, nkb, attend_block, 0)
    for g in range(N_KV_DSA):
        og = (acc_ref[g] / l_ref[g]).astype(o_ref.dtype)
        for r in range(rep):
            hcol = (rep * g + r) * HEAD_DIM
            o_ref[0, :, hcol:hcol + HEAD_DIM] = og[r * tq:(r + 1) * tq]


def dsa_attention(iq, iw, ik, qc, kc, proj3, v_col, topk):
    B, S, _ = qc.shape
    tq = _tile(S, DSA_TQ)
    tk = _tile(S, DSA_TK)
    kvw = N_KV_DSA * HEAD_DIM
    rows = (N_HEADS // N_KV_DSA) * tq
    resident = pl.Buffered(1)
    return pl.pallas_call(
        functools.partial(_dsa_kernel, tq=tq, tk=tk, topk=topk),
        grid=(B, S // tq),
        in_specs=[
            pl.BlockSpec((1, tq, IDX_HEADS * IDX_DIM), lambda b, i: (b, i, 0)),
            pl.BlockSpec((1, tq, IDX_HEADS), lambda b, i: (b, i, 0)),
            pl.BlockSpec((1, S, IDX_DIM), lambda b, i: (b, 0, 0), pipeline_mode=resident),
            pl.BlockSpec((1, tq, BRANCH_W), lambda b, i: (b, i, 0)),
            pl.BlockSpec((1, S, kvw), lambda b, i: (b, 0, 0), pipeline_mode=resident),
            pl.BlockSpec((1, S, kvw), lambda b, i: (b, 0, v_col), pipeline_mode=resident),
        ],
        out_specs=pl.BlockSpec((1, tq, BRANCH_W), lambda b, i: (b, i, 0)),
        out_shape=jax.ShapeDtypeStruct((B, S, BRANCH_W), BF16),
        scratch_shapes=[
            pltpu.VMEM((IDX_HEADS * tq, IDX_DIM), BF16),
            pltpu.VMEM((S // tk, tq, tk), jnp.int32),
            pltpu.VMEM((S // tk, tq, tk), jnp.int16),
            pltpu.VMEM((IDX_HEADS, tq, 128), F32),
            pltpu.VMEM((N_KV_DSA, rows, HEAD_DIM), F32),
            pltpu.VMEM((N_KV_DSA, rows, HEAD_DIM), F32),
            pltpu.VMEM((N_KV_DSA, rows, HEAD_DIM), F32),
        ],
        compiler_params=_cparams(("parallel", "arbitrary")),
        name="dsa_attention",
    )(iq, iw, ik, qc, kc, proj3)


def _split3(c):
    c1 = _trunc_bf16(c)
    r = c - c1
    c2 = _trunc_bf16(r)
    return c1.astype(BF16), c2.astype(BF16), (r - c2).astype(BF16)


def _prep_kernel(fq_ref, fk_ref, dq_ref, iq_ref, dk_ref, misc_ref, const_ref, rope_h_ref,
                 rope_i_ref, pq_ref, pk_ref,
                 qext_ref, kext_ref, qc_ref, kc_ref, iqo_ref, ik_ref, iw_ref, carry_ref,
                 *, tm, tiles_per_seq):
    @pl.when(pl.program_id(0) % tiles_per_seq == 0)
    def _():
        carry_ref[...] = jnp.zeros_like(carry_ref)

    cos_h, sin_h = rope_h_ref[:, :128], rope_h_ref[:, 128:]
    cos_i, sin_i = rope_i_ref[:, :128], rope_i_ref[:, 128:]
    lane = lax.broadcasted_iota(jnp.int32, (tm, 128), 1)
    low_half = (lane & (IDX_DIM - 1)) < IDX_DIM // 2

    def head_norm(x, g):
        return x * lax.rsqrt(jnp.mean(x * x, axis=1, keepdims=True) + NORM_EPS) * g

    def rope128(x):
        return x * cos_h + pltpu.roll(x, 64, 1) * sin_h

    def rope64(x):
        swapped = jnp.where(low_half, pltpu.roll(x, 96, 1), pltpu.roll(x, 32, 1))
        return x * cos_i + swapped * sin_i

    g_fq, g_fk = const_ref[0:1, :], const_ref[1:2, :]
    g_dq, g_dk = const_ref[2:3, :], const_ref[3:4, :]
    misc = misc_ref[...]

    x = misc + const_ref[4:5, :]
    log_f = jnp.minimum(x, 0.0) - jnp.log(1.0 + jnp.exp(-jnp.abs(x)))
    row = lax.broadcasted_iota(jnp.int32, (tm, tm), 0)
    col = lax.broadcasted_iota(jnp.int32, (tm, tm), 1)
    tri = (row >= col).astype(BF16)
    f1, f2, f3 = _split3(log_f)
    c = (_dot(tri, f1) + _dot(tri, f2)) + _dot(tri, f3) + carry_ref[...]
    carry_ref[...] = c[tm - 1:tm, :]
    c1, c2, c3 = _split3(c * LOG2E)
    lane_w = lax.broadcasted_iota(jnp.int32, (tm, BRANCH_W), 1) & (HEAD_DIM - 1)
    qb = (_dot(c1, pq_ref[0]) + _dot(c2, pq_ref[1]) + _dot(c3, pq_ref[2])
          + jnp.where((lane_w >= 3) & (lane_w < 6), 1.0, 0.0))
    kb = (_dot(c1, pk_ref[0]) + _dot(c2, pk_ref[1]) + _dot(c3, pk_ref[2])
          + jnp.where(lane_w < 3, 1.0, 0.0))

    for h in range(N_HEADS):
        sl = slice(h * HEAD_DIM, (h + 1) * HEAD_DIM)
        e0, e1, e2 = 2 * h * HEAD_DIM, (2 * h + 1) * HEAD_DIM, (2 * h + 2) * HEAD_DIM
        qext_ref[:, e0:e1] = head_norm(fq_ref[:, sl].astype(F32), g_fq).astype(BF16)
        qext_ref[:, e1:e2] = qb[:, sl].astype(BF16)
        kext_ref[:, e0:e1] = head_norm(fk_ref[:, sl].astype(F32), g_fk).astype(BF16)
        kext_ref[:, e1:e2] = kb[:, sl].astype(BF16)
        qc_ref[:, sl] = rope128(head_norm(dq_ref[:, sl].astype(F32), g_dq)).astype(BF16)
        iqo_ref[:, sl] = rope64(iq_ref[:, sl].astype(F32)).astype(BF16)
    for g in range(N_KV_DSA):
        sl = slice(g * HEAD_DIM, (g + 1) * HEAD_DIM)
        kc_ref[:, sl] = rope128(head_norm(dk_ref[:, sl].astype(F32), g_dk)).astype(BF16)
    ik_ref[...] = rope64(misc)[:, :IDX_DIM].astype(BF16)
    iw_ref[...] = misc[:, IDX_DIM:IDX_DIM + IDX_HEADS] * (IDX_HEADS ** -0.5 * IDX_DIM ** -0.5)


def prep_attention_operands(proj, misc, consts, rope_h, rope_i, pq, pk, S):
    T = proj.shape[0]
    tm = _tile(S, PREP_TM)
    tps = S // tm
    wide = lambda c: pl.BlockSpec((tm, BRANCH_W), lambda i: (i, c))
    full = lambda a: pl.BlockSpec(a.shape, lambda i: (0,) * a.ndim)
    kvw = N_KV_DSA * HEAD_DIM
    return pl.pallas_call(
        functools.partial(_prep_kernel, tm=tm, tiles_per_seq=tps),
        grid=(T // tm,),
        in_specs=[
            wide(COL_FOX // BRANCH_W), wide(COL_FOX // BRANCH_W + 1),
            wide(COL_DSA_Q // BRANCH_W), wide(COL_IDX_Q // BRANCH_W),
            pl.BlockSpec((tm, kvw), lambda i: (i, COL_DSA_K // kvw)),
            pl.BlockSpec((tm, MISC_W), lambda i: (i, 0)),
            full(consts),
            pl.BlockSpec((tm, 256), lambda i: (i % tps, 0)),
            pl.BlockSpec((tm, 256), lambda i: (i % tps, 0)),
            full(pq), full(pk),
        ],
        out_specs=[
            pl.BlockSpec((tm, 2 * BRANCH_W), lambda i: (i, 0)),
            pl.BlockSpec((tm, 2 * BRANCH_W), lambda i: (i, 0)),
            pl.BlockSpec((tm, BRANCH_W), lambda i: (i, 0)),
            pl.BlockSpec((tm, kvw), lambda i: (i, 0)),
            pl.BlockSpec((tm, BRANCH_W), lambda i: (i, 0)),
            pl.BlockSpec((tm, IDX_DIM), lambda i: (i, 0)),
            pl.BlockSpec((tm, IDX_HEADS), lambda i: (i, 0)),
        ],
        out_shape=[
            jax.ShapeDtypeStruct((T, 2 * BRANCH_W), BF16),
            jax.ShapeDtypeStruct((T, 2 * BRANCH_W), BF16),
            jax.ShapeDtypeStruct((T, BRANCH_W), BF16),
            jax.ShapeDtypeStruct((T, kvw), BF16),
            jax.ShapeDtypeStruct((T, BRANCH_W), BF16),
            jax.ShapeDtypeStruct((T, IDX_DIM), BF16),
            jax.ShapeDtypeStruct((T, IDX_HEADS), F32),
        ],
        scratch_shapes=[pltpu.VMEM((1, MISC_W), F32)],
        compiler_params=_cparams(("arbitrary",)),
        name="prep_attention_operands",
    )(proj, proj, proj, proj, proj, misc, consts, rope_h, rope_i, pq, pk)


def _rope_tables(seq, dim):
    inv = 1.0 / (ROPE_THETA ** (jnp.arange(0, dim, 2, dtype=F32) / dim))
    ang = jnp.arange(seq, dtype=F32)[:, None] * inv[None, :]
    cos, sin = jnp.cos(ang), jnp.sin(ang)
    reps = 128 // dim
    return jnp.concatenate([jnp.tile(jnp.concatenate([cos, cos], axis=1), (1, reps)),
                            jnp.tile(jnp.concatenate([-sin, sin], axis=1), (1, reps))], axis=1)


def _bias_placement():
    pq = np.zeros((3, MISC_W, BRANCH_W), np.float32)
    pk = np.zeros((3, MISC_W, BRANCH_W), np.float32)
    for p in range(3):
        for h in range(N_HEADS):
            pq[p, FOXF_LANE + h, h * HEAD_DIM + p] = 1.0
            pk[p, FOXF_LANE + h, h * HEAD_DIM + 3 + p] = -1.0
    return jnp.asarray(pq, BF16), jnp.asarray(pk, BF16)


def _pack_in_weights(w_in):
    D = w_in.shape[0]
    scale = HEAD_DIM ** -0.5
    o = 0
    sb = w_in[:, o:o + 3 * BRANCH_W]; o += 3 * BRANCH_W
    fox = w_in[:, o:o + 3 * BRANCH_W]; o += 3 * BRANCH_W
    fox_f = w_in[:, o:o + N_HEADS]; o += N_HEADS
    dsa_q = w_in[:, o:o + BRANCH_W]; o += BRANCH_W
    dsa_kv = w_in[:, o:o + 2 * N_KV_DSA * HEAD_DIM]; o += 2 * N_KV_DSA * HEAD_DIM
    idx_q = w_in[:, o:o + IDX_HEADS * IDX_DIM]; o += IDX_HEADS * IDX_DIM
    idx_k = w_in[:, o:o + IDX_DIM]; o += IDX_DIM
    idx_w = w_in[:, o:o + IDX_HEADS]; o += IDX_HEADS
    sb = jnp.concatenate([sb[:, :BRANCH_W] * scale, sb[:, BRANCH_W:]], axis=1)
    w_main = jnp.concatenate([sb, fox, dsa_q, idx_q, dsa_kv], axis=1).astype(BF16)
    pad = jnp.zeros((D, MISC_W - IDX_DIM - IDX_HEADS - N_HEADS), F32)
    misc = jnp.concatenate([idx_k, idx_w, fox_f, pad], axis=1)
    m_hi = _trunc_bf16(misc)
    return w_main, m_hi.astype(BF16), (misc - m_hi).astype(BF16)


def _layer(x2, B, S, rope_h, rope_i, place, topk, norm_mix_g, w_in, fox_f_bias, fox_q_g, fox_k_g,
           dsa_q_g, dsa_k_g, w_gate, w_branch, w_out, norm_ffn_g, w_ffn_gate, w_ffn_up, w_ffn_down):
    T, D = x2.shape
    qscale = HEAD_DIM ** -0.5 * LOG2E
    w_main, wm_hi, wm_lo = _pack_in_weights(w_in)
    proj, h, misc = norm_proj(x2, norm_mix_g, w_main, wm_hi, wm_lo)

    fbias = jnp.zeros((MISC_W,), F32).at[FOXF_LANE:FOXF_LANE + N_HEADS].set(fox_f_bias)
    consts = jnp.zeros((8, MISC_W), F32)
    consts = consts.at[0].set(fox_q_g * qscale).at[1].set(fox_k_g)
    consts = consts.at[2].set(dsa_q_g * qscale).at[3].set(dsa_k_g).at[4].set(fbias)
    q_ext, k_ext, qc, kc, iq, ik, iw = prep_attention_operands(
        proj, misc, consts, rope_h, rope_i, place[0], place[1], S)

    proj3 = proj.reshape(B, S, PROJ_W)
    nblk = BRANCH_W // HEAD_DIM
    seq = lambda a: a.reshape(B, S, a.shape[-1])

    o_sb = sb_attention(proj3, COL_SB // HEAD_DIM, COL_SB // HEAD_DIM + nblk,
                        COL_SB // HEAD_DIM + 2 * nblk)
    o_fox = fox_attention(seq(q_ext), seq(k_ext), proj3, COL_FOX // HEAD_DIM + 2 * nblk)
    o_dsa = dsa_attention(seq(iq), seq(iw), seq(ik), seq(qc), seq(kc), proj3,
                          COL_DSA_V // (N_KV_DSA * HEAD_DIM), topk)

    merged = merge_branches(h, o_sb.reshape(T, -1), o_fox.reshape(T, -1), o_dsa.reshape(T, -1),
                            w_gate.astype(BF16), w_branch.astype(BF16))
    x2 = matmul_residual(merged, w_out.astype(BF16), x2)
    u = ffn_up(x2, norm_ffn_g, w_ffn_gate.astype(BF16), w_ffn_up.astype(BF16))
    return matmul_residual(u, w_ffn_down.astype(BF16), x2)


def kernel(x, norm_mix_g, w_in, fox_f_bias, fox_q_g, fox_k_g, dsa_q_g, dsa_k_g, w_gate, w_branch,
           w_out, norm_ffn_g, w_ffn_gate, w_ffn_up, w_ffn_down):
    B, S, D = x.shape
    topk = min(TOPK_MAX, S // 4)
    rope_h = _rope_tables(S, HEAD_DIM)
    rope_i = _rope_tables(S, IDX_DIM)
    place = _bias_placement()
    x2 = x.reshape(B * S, D)
    for l in range(norm_mix_g.shape[0]):
        x2 = _layer(x2, B, S, rope_h, rope_i, place, topk, norm_mix_g[l], w_in[l], fox_f_bias[l],
                    fox_q_g[l], fox_k_g[l], dsa_q_g[l], dsa_k_g[l], w_gate[l], w_branch[l],
                    w_out[l], norm_ffn_g[l], w_ffn_gate[l], w_ffn_up[l], w_ffn_down[l])
    return x2.reshape(B, S, D)
```

```python
import functools

import jax
import jax.numpy as jnp
import numpy as np
from jax import lax
from jax.experimental import pallas as pl
from jax.experimental.pallas import tpu as pltpu

F32 = jnp.float32
BF16 = jnp.bfloat16

HEAD_DIM = 128
N_HEADS = 8
N_KV_DSA = 2
IDX_HEADS = 16
IDX_DIM = 64
TOPK_MAX = 256
ROPE_THETA = 10000.0
NORM_EPS = 1e-6
BRANCH_W = N_HEADS * HEAD_DIM
MISC_W = 128
FOXF_LANE = IDX_DIM + IDX_HEADS
COL_SB = 0
COL_FOX = 3 * BRANCH_W
COL_DSA_Q = 6 * BRANCH_W
COL_IDX_Q = 7 * BRANCH_W
COL_DSA_K = 8 * BRANCH_W
COL_DSA_V = COL_DSA_K + N_KV_DSA * HEAD_DIM
PROJ_W = COL_DSA_V + N_KV_DSA * HEAD_DIM
PREP_TM = 256

LOG2E = 1.4426950408889634
SB_T, SB_SUB = 256, 4
FOX_TQ, FOX_TK = 1024, 512
DSA_TQ, DSA_TK = 256, 512
IDX_GROUP = 4
NEG = -1e30
INT_MIN = -(2 ** 31)
SB_SKIP = 100.0
FOX_SKIP = 152.0
VMEM_LIMIT_BYTES = 56 * 1024 * 1024


def _cparams(sem):
    return pltpu.CompilerParams(dimension_semantics=sem, vmem_limit_bytes=VMEM_LIMIT_BYTES)


def _tile(n, pref):
    t = pref
    while n % t:
        t //= 2
    return t


def _dot(a, b):
    return jnp.dot(a, b, preferred_element_type=F32)


def _dot_nt(a, b):
    return lax.dot_general(a, b, (((1,), (1,)), ((), ())), preferred_element_type=F32)


def _rep(x, n):
    return x if n == 1 else jnp.concatenate([x] * n, axis=1)


def _rmsnorm_f32(x, g):
    return x * lax.rsqrt(jnp.mean(x * x, axis=-1, keepdims=True) + NORM_EPS) * g


def _trunc_bf16(x):
    bits = lax.bitcast_convert_type(x, jnp.uint32) & jnp.uint32(0xFFFF0000)
    return lax.bitcast_convert_type(bits, F32)


def _norm_proj_kernel(x_ref, g_ref, w_ref, wmh_ref, wml_ref, proj_ref, h_ref, misc_ref, hs_ref):
    @pl.when(pl.program_id(1) == 0)
    def _():
        y = _rmsnorm_f32(x_ref[...], g_ref[...])
        hb = y.astype(BF16)
        hs_ref[...] = hb
        h_ref[...] = hb
        y_hi = _trunc_bf16(y)
        hi = y_hi.astype(BF16)
        lo = (y - y_hi).astype(BF16)
        misc_ref[...] = (_dot(hi, wmh_ref[...]) + _dot(hi, wml_ref[...])) + _dot(lo, wmh_ref[...])

    proj_ref[...] = _dot(hs_ref[...], w_ref[...]).astype(proj_ref.dtype)


def norm_proj(x, g, w, wm_hi, wm_lo):
    T, D = x.shape
    N = w.shape[1]
    tm, tn = _tile(T, 1024), _tile(N, 512)
    return pl.pallas_call(
        _norm_proj_kernel,
        grid=(T // tm, N // tn),
        in_specs=[
            pl.BlockSpec((tm, D), lambda i, j: (i, 0)),
            pl.BlockSpec((1, D), lambda i, j: (0, 0)),
            pl.BlockSpec((D, tn), lambda i, j: (0, j)),
            pl.BlockSpec((D, MISC_W), lambda i, j: (0, 0)),
            pl.BlockSpec((D, MISC_W), lambda i, j: (0, 0)),
        ],
        out_specs=[
            pl.BlockSpec((tm, tn), lambda i, j: (i, j)),
            pl.BlockSpec((tm, D), lambda i, j: (i, 0)),
            pl.BlockSpec((tm, MISC_W), lambda i, j: (i, 0)),
        ],
        out_shape=[
            jax.ShapeDtypeStruct((T, N), BF16),
            jax.ShapeDtypeStruct((T, D), BF16),
            jax.ShapeDtypeStruct((T, MISC_W), F32),
        ],
        scratch_shapes=[pltpu.VMEM((tm, D), BF16)],
        compiler_params=_cparams(("parallel", "arbitrary")),
        name="norm_proj",
    )(x, g.reshape(1, D), w, wm_hi, wm_lo)


def _merge_kernel(h_ref, o0_ref, o1_ref, o2_ref, wg_ref, wb_ref, out_ref):
    h = h_ref[...]
    acc = None
    for i, o_ref in enumerate((o0_ref, o1_ref, o2_ref)):
        y = jax.nn.sigmoid(_dot(h, wg_ref[i])) * _dot(o_ref[...], wb_ref[i])
        acc = y if acc is None else acc + y
    out_ref[...] = acc.astype(out_ref.dtype)


def merge_branches(h, o_sb, o_fox, o_dsa, wg, wb):
    T, D = h.shape
    W = o_sb.shape[1]
    N = wg.shape[2]
    tm, tn = _tile(T, 512), _tile(N, 512)
    o_spec = pl.BlockSpec((tm, W), lambda i, j: (i, 0))
    return pl.pallas_call(
        _merge_kernel,
        grid=(T // tm, N // tn),
        in_specs=[
            pl.BlockSpec((tm, D), lambda i, j: (i, 0)),
            o_spec, o_spec, o_spec,
            pl.BlockSpec((3, D, tn), lambda i, j: (0, 0, j)),
            pl.BlockSpec((3, W, tn), lambda i, j: (0, 0, j)),
        ],
        out_specs=pl.BlockSpec((tm, tn), lambda i, j: (i, j)),
        out_shape=jax.ShapeDtypeStruct((T, N), BF16),
        compiler_params=_cparams(("parallel", "arbitrary")),
        name="merge_branches",
    )(h, o_sb, o_fox, o_dsa, wg, wb)


def _matmul_residual_kernel(a_ref, b_ref, r_ref, o_ref):
    o_ref[...] = r_ref[...] + _dot(a_ref[...], b_ref[...])


def matmul_residual(a, b, r, tile=(512, 512)):
    T, K = a.shape
    N = b.shape[1]
    tm, tn = _tile(T, tile[0]), _tile(N, tile[1])
    return pl.pallas_call(
        _matmul_residual_kernel,
        grid=(T // tm, N // tn),
        in_specs=[
            pl.BlockSpec((tm, K), lambda i, j: (i, 0)),
            pl.BlockSpec((K, tn), lambda i, j: (0, j)),
            pl.BlockSpec((tm, tn), lambda i, j: (i, j)),
        ],
        out_specs=pl.BlockSpec((tm, tn), lambda i, j: (i, j)),
        out_shape=jax.ShapeDtypeStruct((T, N), F32),
        compiler_params=_cparams(("parallel", "arbitrary")),
        name="matmul_residual",
    )(a, b, r)


def _ffn_up_kernel(x_ref, g_ref, wg_ref, wu_ref, u_ref, hs_ref):
    @pl.when(pl.program_id(1) == 0)
    def _():
        hs_ref[...] = _rmsnorm_f32(x_ref[...], g_ref[...]).astype(BF16)

    h = hs_ref[...]
    u_ref[...] = (jax.nn.silu(_dot(h, wg_ref[...])) * _dot(h, wu_ref[...])).astype(u_ref.dtype)


def ffn_up(x, g, wg, wu):
    T, D = x.shape
    Fd = wg.shape[1]
    tm, tn = _tile(T, 1024), _tile(Fd, 512)
    return pl.pallas_call(
        _ffn_up_kernel,
        grid=(T // tm, Fd // tn),
        in_specs=[
            pl.BlockSpec((tm, D), lambda i, j: (i, 0)),
            pl.BlockSpec((1, D), lambda i, j: (0, 0)),
            pl.BlockSpec((D, tn), lambda i, j: (0, j)),
            pl.BlockSpec((D, tn), lambda i, j: (0, j)),
        ],
        out_specs=pl.BlockSpec((tm, tn), lambda i, j: (i, j)),
        out_shape=jax.ShapeDtypeStruct((T, Fd), BF16),
        scratch_shapes=[pltpu.VMEM((tm, D), BF16)],
        compiler_params=_cparams(("parallel", "arbitrary")),
        name="ffn_up",
    )(x, g.reshape(1, D), wg, wu)


def _online_softmax_update(s, v, m_ref, l_ref, acc_ref, idx):
    m_prev = m_ref[idx]
    m_new = jnp.maximum(m_prev, jnp.max(s, axis=1, keepdims=True))
    p = jnp.exp2(s - _rep(m_new, s.shape[1] // 128))
    alpha = jnp.exp2(m_prev - m_new)
    l_ref[idx] = alpha * l_ref[idx] + jnp.sum(p, axis=1, keepdims=True)
    acc_ref[idx] = alpha * acc_ref[idx] + _dot(p.astype(BF16), v)
    m_ref[idx] = m_new


def _sb_kernel(q_ref, k_ref, v_ref, o_ref, acc_ref, car_ref, *, t, n_sub):
    base = pl.program_id(2) * n_sub
    row = lax.broadcasted_iota(jnp.int32, (t, t), 0)
    col = lax.broadcasted_iota(jnp.int32, (t, t), 1)
    upper = (row > col).astype(BF16)
    ones = jnp.ones((t, 128), BF16)
    strict = col < row

    def process(g, j, diag, valid):
        start = pl.multiple_of(j * t, t)
        k = k_ref[0, pl.ds(start, t), :]
        v = v_ref[0, pl.ds(start, t), :]
        z = _dot_nt(q_ref[0, g * t:(g + 1) * t, :], k)
        log_keep = -(jnp.maximum(z, 0.0) + jnp.log(1.0 + jnp.exp(-jnp.abs(z))))
        lk = jnp.where(strict, log_keep, 0.0) if diag else log_keep
        hi_f = _trunc_bf16(lk)
        hi = hi_f.astype(BF16)
        lo = (lk - hi_f).astype(BF16)
        after = _dot(hi, upper) + _dot(lo, upper)
        total = _dot(hi, ones) + _dot(lo, ones)
        car = car_ref[g]
        a = jnp.exp(z + log_keep + after + _rep(car, t // 128))
        if diag:
            a = jnp.where(strict, a, 0.0)
        if valid is not None:
            a = jnp.where(valid, a, 0.0)
            total = jnp.where(valid, total, 0.0)
        acc_ref[g] += _dot(a.astype(BF16), v)
        car = car + total
        car_ref[g] = car
        return jnp.max(car)

    acc_ref[...] = jnp.zeros_like(acc_ref)
    car_ref[...] = jnp.zeros_like(car_ref)
    peak = [process(g, base + g, True, None) for g in range(n_sub)]

    def more(depth, peak):
        need = [jnp.logical_and(base + g - depth >= 0, peak[g] > -SB_SKIP) for g in range(n_sub)]
        return functools.reduce(jnp.logical_or, need)

    def body(st):
        depth, _ = st
        peak = []
        for g in range(n_sub):
            j = base + g - depth
            peak.append(process(g, jnp.maximum(j, 0), False, j >= 0))
        return depth + 1, more(depth + 1, peak)

    lax.while_loop(lambda st: st[1], body, (jnp.int32(1), more(1, peak)))
    for g in range(n_sub):
        o_ref[0, g * t:(g + 1) * t, :] = acc_ref[g].astype(o_ref.dtype)


def sb_attention(proj3, q_col, k_col, v_col):
    B, S, _ = proj3.shape
    t = _tile(S, SB_T)
    n_sub = _tile(S // t, SB_SUB)
    rows = t * n_sub
    return pl.pallas_call(
        functools.partial(_sb_kernel, t=t, n_sub=n_sub),
        grid=(B, N_HEADS, S // rows),
        in_specs=[
            pl.BlockSpec((1, rows, HEAD_DIM), lambda b, h, i: (b, i, q_col + h)),
            pl.BlockSpec((1, S, HEAD_DIM), lambda b, h, i: (b, 0, k_col + h)),
            pl.BlockSpec((1, S, HEAD_DIM), lambda b, h, i: (b, 0, v_col + h)),
        ],
        out_specs=pl.BlockSpec((1, rows, HEAD_DIM), lambda b, h, i: (b, i, h)),
        out_shape=jax.ShapeDtypeStruct((B, S, BRANCH_W), BF16),
        scratch_shapes=[pltpu.VMEM((n_sub, t, HEAD_DIM), F32), pltpu.VMEM((n_sub, t, HEAD_DIM), F32)],
        compiler_params=_cparams(("parallel", "parallel", "arbitrary")),
        name="sb_attention",
    )(proj3, proj3, proj3)


def _fox_kernel(cb_ref, thr_ref, q_ref, k_ref, v_ref, o_ref, m_ref, l_ref, acc_ref,
                *, tq, tk, tile):
    i = pl.program_id(2)
    q = q_ref[0]
    m_ref[...] = jnp.full_like(m_ref, NEG)
    l_ref[...] = jnp.zeros_like(l_ref)
    acc_ref[...] = jnp.zeros_like(acc_ref)
    row = lax.broadcasted_iota(jnp.int32, (tq, tk), 0)
    col = lax.broadcasted_iota(jnp.int32, (tq, tk), 1)

    def step(start, mask):
        start = pl.multiple_of(start, tk)
        s = _dot_nt(q, k_ref[0, pl.ds(start, tk), :])
        if mask is not None:
            s = jnp.where(mask, s, NEG)
        _online_softmax_update(s, v_ref[0, pl.ds(start, tk), :], m_ref, l_ref, acc_ref, 0)

    def body(j, c):
        step(j * tk, None)
        return c

    head = pl.program_id(0) * N_HEADS + pl.program_id(1)
    c_first = cb_ref[head, 2 * (i * (tq // tile))]
    n_before = i * (tq // tk)

    def negligible(j):
        c_last = cb_ref[head, 2 * ((j + 1) * (tk // tile) - 1) + 1]
        return jnp.logical_and(j < n_before, c_first - c_last < -thr_ref[0])

    first = lax.while_loop(negligible, lambda j: j + 1, jnp.int32(0))
    lax.fori_loop(first, n_before, body, 0)
    for d in range(tq // tk):
        step(i * tq + d * tk, row >= col + d * tk)
    o_ref[0] = (acc_ref[0] / l_ref[0]).astype(o_ref.dtype)


def fox_attention(q_ext, k_ext, proj3, v_col, cb, thr, tile):
    B, S, _ = q_ext.shape
    tq = _tile(S, FOX_TQ)
    tk = _tile(tq, FOX_TK)
    E = 2 * HEAD_DIM
    smem = pl.BlockSpec(memory_space=pltpu.SMEM)
    return pl.pallas_call(
        functools.partial(_fox_kernel, tq=tq, tk=tk, tile=tile),
        grid=(B, N_HEADS, S // tq),
        in_specs=[
            smem, smem,
            pl.BlockSpec((1, tq, E), lambda b, h, i: (b, i, h)),
            pl.BlockSpec((1, S, E), lambda b, h, i: (b, 0, h)),
            pl.BlockSpec((1, S, HEAD_DIM), lambda b, h, i: (b, 0, v_col + h)),
        ],
        out_specs=pl.BlockSpec((1, tq, HEAD_DIM), lambda b, h, i: (b, i, h)),
        out_shape=jax.ShapeDtypeStruct((B, S, BRANCH_W), BF16),
        scratch_shapes=[pltpu.VMEM((1, tq, HEAD_DIM), F32)] * 3,
        compiler_params=_cparams(("parallel", "parallel", "arbitrary")),
        name="fox_attention",
    )(cb, thr, q_ext, k_ext, proj3)


def _dsa_kernel(iq_ref, w_ref, ik_ref, qc_ref, kc_ref, vc_ref, o_ref,
                iqs_ref, key_ref, k16_ref, wrep_ref, m_ref, l_ref, acc_ref, *, tq, tk, topk):
    i = pl.program_id(1)
    nkb = (i * tq + tq + tk - 1) // tk
    rep = N_HEADS // N_KV_DSA
    nrep = tk // 128
    kf = float(topk)

    w = w_ref[0]
    for h in range(IDX_HEADS):
        wrep_ref[h] = jnp.broadcast_to(w[:, h:h + 1], (tq, 128))
        iqs_ref[h * tq:(h + 1) * tq, :] = iq_ref[0, :, h * IDX_DIM:(h + 1) * IDX_DIM]
    rowpos = i * tq + lax.broadcasted_iota(jnp.int32, (tq, tk), 0)
    colio = lax.broadcasted_iota(jnp.int32, (tq, tk), 1)

    def score_block(jk, c):
        start = pl.multiple_of(jk * tk, tk)
        ikb = ik_ref[0, pl.ds(start, tk), :]
        score = jnp.zeros((tq, tk), F32)
        for hg in range(IDX_HEADS // IDX_GROUP):
            z = _dot_nt(iqs_ref[hg * IDX_GROUP * tq:(hg + 1) * IDX_GROUP * tq, :], ikb)
            for r in range(IDX_GROUP):
                score = score + (jnp.maximum(z[r * tq:(r + 1) * tq], 0.0)
                                 * _rep(wrep_ref[hg * IDX_GROUP + r], nrep))
        bits = lax.bitcast_convert_type(score, jnp.int32)
        key = bits ^ ((bits >> 31) & jnp.int32(0x7FFFFFFF))
        key = jnp.where(start + colio <= rowpos, key, jnp.int32(INT_MIN))
        key_ref[jk] = key
        k16_ref[jk] = (key >> 16).astype(jnp.int16)
        return c

    lax.fori_loop(0, nkb, score_block, 0)

    def count_ge(cand):
        cw = _rep(cand.astype(jnp.int16), nrep)

        def blk(jk, c):
            hit = jnp.where(k16_ref[jk] >= cw, jnp.int16(1), jnp.int16(0))
            for t in range(nrep):
                c = c + hit[:, t * 128:(t + 1) * 128]
            return c

        c = lax.fori_loop(0, nkb, blk, jnp.zeros((tq, 128), jnp.int16))
        cnt = jnp.sum(c.astype(jnp.int32).astype(F32), axis=1, keepdims=True)
        return jnp.broadcast_to(cnt, (tq, 128))

    def bisect16(need, count_all):
        def bit(it, st):
            tau, cge = st
            cand = tau + lax.shift_left(jnp.int32(1), 15 - it)
            cnt = count_ge(cand)
            ok = cnt >= need
            return jnp.where(ok, cand, tau), jnp.where(ok, cnt, cge)

        return lax.fori_loop(0, 16, bit, (jnp.full((tq, 128), -32768, jnp.int32), count_all))

    n_all = jnp.zeros((tq, 128), F32) + (nkb * tk).astype(F32)
    tau_hi, cge_hi = bisect16(kf, n_all)
    n_gt = count_ge(tau_hi + 1)
    t16w = _rep(tau_hi.astype(jnp.int16), nrep)

    def low_block(jk, c):
        lo = ((key_ref[jk] & jnp.int32(0xFFFF)) - 32768).astype(jnp.int16)
        k16_ref[jk] = jnp.where(k16_ref[jk] == t16w, lo, jnp.int16(-32768))
        return c

    lax.fori_loop(0, nkb, low_block, 0)
    tau_lo, cge_lo = bisect16(kf - n_gt, cge_hi - n_gt)
    tau = jnp.maximum(lax.shift_left(tau_hi, 16) | (tau_lo + 32768), jnp.int32(INT_MIN + 1))
    tauw = _rep(tau, nrep)

    surplus = (n_gt + cge_lo) - kf

    @pl.when(jnp.max(surplus) > 0.0)
    def _():
        def count(pred):
            def blk(jk, c):
                hit = jnp.where(pred(key_ref[jk], jk * tk + colio), 1.0, 0.0)
                for t in range(nrep):
                    c = c + hit[:, t * 128:(t + 1) * 128]
                return c

            c = lax.fori_loop(0, nkb, blk, jnp.zeros((tq, 128), F32))
            return jnp.broadcast_to(jnp.sum(c, axis=1, keepdims=True), (tq, 128))

        keep = kf - count(lambda key, pos: key > tauw)
        nbits = (key_ref.shape[0] * tk - 1).bit_length()

        def bit(it, last):
            cand = last + lax.shift_left(jnp.int32(1), nbits - 1 - it)
            candw = _rep(cand, nrep)
            before = count(lambda key, pos: jnp.logical_and(key == tauw, pos < candw))
            return jnp.where(before < keep, cand, last)

        last = lax.fori_loop(0, nbits, bit, jnp.zeros((tq, 128), jnp.int32))
        lastw = _rep(jnp.where(surplus > 0.0, last, jnp.int32(2 ** 30)), nrep)

        def drop(jk, c):
            key = key_ref[jk]
            late_tie = jnp.logical_and(key == tauw, jk * tk + colio > lastw)
            key_ref[jk] = jnp.where(late_tie, jnp.int32(INT_MIN), key)
            return c

        lax.fori_loop(0, nkb, drop, 0)

    m_ref[...] = jnp.full_like(m_ref, NEG)
    l_ref[...] = jnp.zeros_like(l_ref)
    acc_ref[...] = jnp.zeros_like(acc_ref)
    qc = qc_ref[0]
    qg = [jnp.concatenate([qc[:, (rep * g + r) * HEAD_DIM:(rep * g + r + 1) * HEAD_DIM]
                           for r in range(rep)], axis=0) for g in range(N_KV_DSA)]

    def attend_block(jk, c):
        start = pl.multiple_of(jk * tk, tk)
        bias = jnp.where(key_ref[jk] >= tauw, 0.0, NEG)
        bias = jnp.concatenate([bias] * rep, axis=0)
        for g in range(N_KV_DSA):
            kg = kc_ref[0, pl.ds(start, tk), g * HEAD_DIM:(g + 1) * HEAD_DIM]
            vg = vc_ref[0, pl.ds(start, tk), g * HEAD_DIM:(g + 1) * HEAD_DIM]
            _online_softmax_update(_dot_nt(qg[g], kg) + bias, vg, m_ref, l_ref, acc_ref, g)
        return c

    lax.fori_loop(0, nkb, attend_block, 0)
    for g in range(N_KV_DSA):
        og = (acc_ref[g] / l_ref[g]).astype(o_ref.dtype)
        for r in range(rep):
            hcol = (rep * g + r) * HEAD_DIM
            o_ref[0, :, hcol:hcol + HEAD_DIM] = og[r * tq:(r + 1) * tq]


def dsa_attention(iq, iw, ik, qc, kc, proj3, v_col, topk):
    B, S, _ = qc.shape
    tq = _tile(S, DSA_TQ)
    tk = _tile(S, DSA_TK)
    kvw = N_KV_DSA * HEAD_DIM
    rows = (N_HEADS // N_KV_DSA) * tq
    resident = pl.Buffered(1)
    return pl.pallas_call(
        functools.partial(_dsa_kernel, tq=tq, tk=tk, topk=topk),
        grid=(B, S // tq),
        in_specs=[
            pl.BlockSpec((1, tq, IDX_HEADS * IDX_DIM), lambda b, i: (b, i, 0)),
            pl.BlockSpec((1, tq, IDX_HEADS), lambda b, i: (b, i, 0)),
            pl.BlockSpec((1, S, IDX_DIM), lambda b, i: (b, 0, 0), pipeline_mode=resident),
            pl.BlockSpec((1, tq, BRANCH_W), lambda b, i: (b, i, 0)),
            pl.BlockSpec((1, S, kvw), lambda b, i: (b, 0, 0), pipeline_mode=resident),
            pl.BlockSpec((1, S, kvw), lambda b, i: (b, 0, v_col), pipeline_mode=resident),
        ],
        out_specs=pl.BlockSpec((1, tq, BRANCH_W), lambda b, i: (b, i, 0)),
        out_shape=jax.ShapeDtypeStruct((B, S, BRANCH_W), BF16),
        scratch_shapes=[
            pltpu.VMEM((IDX_HEADS * tq, IDX_DIM), BF16),
            pltpu.VMEM((S // tk, tq, tk), jnp.int32),
            pltpu.VMEM((S // tk, tq, tk), jnp.int16),
            pltpu.VMEM((IDX_HEADS, tq, 128), F32),
            pltpu.VMEM((N_KV_DSA, rows, HEAD_DIM), F32),
            pltpu.VMEM((N_KV_DSA, rows, HEAD_DIM), F32),
            pltpu.VMEM((N_KV_DSA, rows, HEAD_DIM), F32),
        ],
        compiler_params=_cparams(("parallel", "arbitrary")),
        name="dsa_attention",
    )(iq, iw, ik, qc, kc, proj3)


def _split3(c):
    c1 = _trunc_bf16(c)
    r = c - c1
    c2 = _trunc_bf16(r)
    return c1.astype(BF16), c2.astype(BF16), (r - c2).astype(BF16)


def _prep_kernel(fq_ref, fk_ref, dq_ref, iq_ref, dk_ref, misc_ref, const_ref, rope_h_ref,
                 rope_i_ref, pq_ref, pk_ref,
                 qext_ref, kext_ref, qc_ref, kc_ref, iqo_ref, ik_ref, iw_ref, cb_ref, carry_ref,
                 *, tm, tiles_per_seq):
    @pl.when(pl.program_id(0) % tiles_per_seq == 0)
    def _():
        carry_ref[...] = jnp.zeros_like(carry_ref)

    cos_h, sin_h = rope_h_ref[:, :128], rope_h_ref[:, 128:]
    cos_i, sin_i = rope_i_ref[:, :128], rope_i_ref[:, 128:]
    lane = lax.broadcasted_iota(jnp.int32, (tm, 128), 1)
    low_half = (lane & (IDX_DIM - 1)) < IDX_DIM // 2

    def head_norm(x, g):
        return x * lax.rsqrt(jnp.mean(x * x, axis=1, keepdims=True) + NORM_EPS) * g

    def rope128(x):
        return x * cos_h + pltpu.roll(x, 64, 1) * sin_h

    def rope64(x):
        swapped = jnp.where(low_half, pltpu.roll(x, 96, 1), pltpu.roll(x, 32, 1))
        return x * cos_i + swapped * sin_i

    g_fq, g_fk = const_ref[0:1, :], const_ref[1:2, :]
    g_dq, g_dk = const_ref[2:3, :], const_ref[3:4, :]
    misc = misc_ref[...]

    x = misc + const_ref[4:5, :]
    log_f = jnp.minimum(x, 0.0) - jnp.log(1.0 + jnp.exp(-jnp.abs(x)))
    row = lax.broadcasted_iota(jnp.int32, (tm, tm), 0)
    col = lax.broadcasted_iota(jnp.int32, (tm, tm), 1)
    tri = (row >= col).astype(BF16)
    f1, f2, f3 = _split3(log_f)
    c = (_dot(tri, f1) + _dot(tri, f2)) + _dot(tri, f3) + carry_ref[...]
    carry_ref[...] = c[tm - 1:tm, :]
    c = c * LOG2E
    cb_ref[0, 0:1, :] = c[0:1, :]
    cb_ref[0, 1:2, :] = c[tm - 1:tm, :]
    c1, c2, c3 = _split3(c)
    lane_w = lax.broadcasted_iota(jnp.int32, (tm, BRANCH_W), 1) & (HEAD_DIM - 1)
    qb = (_dot(c1, pq_ref[0]) + _dot(c2, pq_ref[1]) + _dot(c3, pq_ref[2])
          + jnp.where((lane_w >= 3) & (lane_w < 6), 1.0, 0.0))
    kb = (_dot(c1, pk_ref[0]) + _dot(c2, pk_ref[1]) + _dot(c3, pk_ref[2])
          + jnp.where(lane_w < 3, 1.0, 0.0))

    for h in range(N_HEADS):
        sl = slice(h * HEAD_DIM, (h + 1) * HEAD_DIM)
        e0, e1, e2 = 2 * h * HEAD_DIM, (2 * h + 1) * HEAD_DIM, (2 * h + 2) * HEAD_DIM
        qext_ref[:, e0:e1] = head_norm(fq_ref[:, sl].astype(F32), g_fq).astype(BF16)
        qext_ref[:, e1:e2] = qb[:, sl].astype(BF16)
        kext_ref[:, e0:e1] = head_norm(fk_ref[:, sl].astype(F32), g_fk).astype(BF16)
        kext_ref[:, e1:e2] = kb[:, sl].astype(BF16)
        qc_ref[:, sl] = rope128(head_norm(dq_ref[:, sl].astype(F32), g_dq)).astype(BF16)
        iqo_ref[:, sl] = rope64(iq_ref[:, sl].astype(F32)).astype(BF16)
    for g in range(N_KV_DSA):
        sl = slice(g * HEAD_DIM, (g + 1) * HEAD_DIM)
        kc_ref[:, sl] = rope128(head_norm(dk_ref[:, sl].astype(F32), g_dk)).astype(BF16)
    ik_ref[...] = rope64(misc)[:, :IDX_DIM].astype(BF16)
    iw_ref[...] = misc[:, IDX_DIM:IDX_DIM + IDX_HEADS] * (IDX_HEADS ** -0.5 * IDX_DIM ** -0.5)


def prep_attention_operands(proj, misc, consts, rope_h, rope_i, pq, pk, S):
    T = proj.shape[0]
    tm = _tile(S, PREP_TM)
    tps = S // tm
    wide = lambda c: pl.BlockSpec((tm, BRANCH_W), lambda i: (i, c))
    full = lambda a: pl.BlockSpec(a.shape, lambda i: (0,) * a.ndim)
    kvw = N_KV_DSA * HEAD_DIM
    return pl.pallas_call(
        functools.partial(_prep_kernel, tm=tm, tiles_per_seq=tps),
        grid=(T // tm,),
        in_specs=[
            wide(COL_FOX // BRANCH_W), wide(COL_FOX // BRANCH_W + 1),
            wide(COL_DSA_Q // BRANCH_W), wide(COL_IDX_Q // BRANCH_W),
            pl.BlockSpec((tm, kvw), lambda i: (i, COL_DSA_K // kvw)),
            pl.BlockSpec((tm, MISC_W), lambda i: (i, 0)),
            full(consts),
            pl.BlockSpec((tm, 256), lambda i: (i % tps, 0)),
            pl.BlockSpec((tm, 256), lambda i: (i % tps, 0)),
            full(pq), full(pk),
        ],
        out_specs=[
            pl.BlockSpec((tm, 2 * BRANCH_W), lambda i: (i, 0)),
            pl.BlockSpec((tm, 2 * BRANCH_W), lambda i: (i, 0)),
            pl.BlockSpec((tm, BRANCH_W), lambda i: (i, 0)),
            pl.BlockSpec((tm, kvw), lambda i: (i, 0)),
            pl.BlockSpec((tm, BRANCH_W), lambda i: (i, 0)),
            pl.BlockSpec((tm, IDX_DIM), lambda i: (i, 0)),
            pl.BlockSpec((tm, IDX_HEADS), lambda i: (i, 0)),
            pl.BlockSpec((1, 2, MISC_W), lambda i: (i, 0, 0)),
        ],
        out_shape=[
            jax.ShapeDtypeStruct((T, 2 * BRANCH_W), BF16),
            jax.ShapeDtypeStruct((T, 2 * BRANCH_W), BF16),
            jax.ShapeDtypeStruct((T, BRANCH_W), BF16),
            jax.ShapeDtypeStruct((T, kvw), BF16),
            jax.ShapeDtypeStruct((T, BRANCH_W), BF16),
            jax.ShapeDtypeStruct((T, IDX_DIM), BF16),
            jax.ShapeDtypeStruct((T, IDX_HEADS), F32),
            jax.ShapeDtypeStruct((T // tm, 2, MISC_W), F32),
        ],
        scratch_shapes=[pltpu.VMEM((1, MISC_W), F32)],
        compiler_params=_cparams(("arbitrary",)),
        name="prep_attention_operands",
    )(proj, proj, proj, proj, proj, misc, consts, rope_h, rope_i, pq, pk)


def _rope_tables(seq, dim):
    inv = 1.0 / (ROPE_THETA ** (jnp.arange(0, dim, 2, dtype=F32) / dim))
    ang = jnp.arange(seq, dtype=F32)[:, None] * inv[None, :]
    cos, sin = jnp.cos(ang), jnp.sin(ang)
    reps = 128 // dim
    return jnp.concatenate([jnp.tile(jnp.concatenate([cos, cos], axis=1), (1, reps)),
                            jnp.tile(jnp.concatenate([-sin, sin], axis=1), (1, reps))], axis=1)


def _bias_placement():
    pq = np.zeros((3, MISC_W, BRANCH_W), np.float32)
    pk = np.zeros((3, MISC_W, BRANCH_W), np.float32)
    for p in range(3):
        for h in range(N_HEADS):
            pq[p, FOXF_LANE + h, h * HEAD_DIM + p] = 1.0
            pk[p, FOXF_LANE + h, h * HEAD_DIM + 3 + p] = -1.0
    return jnp.asarray(pq, BF16), jnp.asarray(pk, BF16)


def _pack_in_weights(w_in):
    D = w_in.shape[0]
    scale = HEAD_DIM ** -0.5
    o = 0
    sb = w_in[:, o:o + 3 * BRANCH_W]; o += 3 * BRANCH_W
    fox = w_in[:, o:o + 3 * BRANCH_W]; o += 3 * BRANCH_W
    fox_f = w_in[:, o:o + N_HEADS]; o += N_HEADS
    dsa_q = w_in[:, o:o + BRANCH_W]; o += BRANCH_W
    dsa_kv = w_in[:, o:o + 2 * N_KV_DSA * HEAD_DIM]; o += 2 * N_KV_DSA * HEAD_DIM
    idx_q = w_in[:, o:o + IDX_HEADS * IDX_DIM]; o += IDX_HEADS * IDX_DIM
    idx_k = w_in[:, o:o + IDX_DIM]; o += IDX_DIM
    idx_w = w_in[:, o:o + IDX_HEADS]; o += IDX_HEADS
    sb = jnp.concatenate([sb[:, :BRANCH_W] * scale, sb[:, BRANCH_W:]], axis=1)
    w_main = jnp.concatenate([sb, fox, dsa_q, idx_q, dsa_kv], axis=1).astype(BF16)
    pad = jnp.zeros((D, MISC_W - IDX_DIM - IDX_HEADS - N_HEADS), F32)
    misc = jnp.concatenate([idx_k, idx_w, fox_f, pad], axis=1)
    m_hi = _trunc_bf16(misc)
    return w_main, m_hi.astype(BF16), (misc - m_hi).astype(BF16)


def _layer(x2, B, S, rope_h, rope_i, place, topk, norm_mix_g, w_in, fox_f_bias, fox_q_g, fox_k_g,
           dsa_q_g, dsa_k_g, w_gate, w_branch, w_out, norm_ffn_g, w_ffn_gate, w_ffn_up, w_ffn_down):
    T, D = x2.shape
    qscale = HEAD_DIM ** -0.5 * LOG2E
    w_main, wm_hi, wm_lo = _pack_in_weights(w_in)
    proj, h, misc = norm_proj(x2, norm_mix_g, w_main, wm_hi, wm_lo)

    fbias = jnp.zeros((MISC_W,), F32).at[FOXF_LANE:FOXF_LANE + N_HEADS].set(fox_f_bias)
    consts = jnp.zeros((8, MISC_W), F32)
    consts = consts.at[0].set(fox_q_g * qscale).at[1].set(fox_k_g)
    consts = consts.at[2].set(dsa_q_g * qscale).at[3].set(dsa_k_g).at[4].set(fbias)
    q_ext, k_ext, qc, kc, iq, ik, iw, cb = prep_attention_operands(
        proj, misc, consts, rope_h, rope_i, place[0], place[1], S)
    tile = _tile(S, PREP_TM)
    cb = cb.reshape(B, S // tile, 2, MISC_W)[..., FOXF_LANE:FOXF_LANE + N_HEADS]
    cb = cb.transpose(0, 3, 1, 2).reshape(B * N_HEADS, 2 * (S // tile))
    zmax = 1.02 * HEAD_DIM * jnp.max(jnp.abs(fox_q_g * qscale)) * jnp.max(jnp.abs(fox_k_g))
    thr = (2.0 * zmax + FOX_SKIP).reshape(1)

    proj3 = proj.reshape(B, S, PROJ_W)
    nblk = BRANCH_W // HEAD_DIM
    seq = lambda a: a.reshape(B, S, a.shape[-1])

    o_sb = sb_attention(proj3, COL_SB // HEAD_DIM, COL_SB // HEAD_DIM + nblk,
                        COL_SB // HEAD_DIM + 2 * nblk)
    o_fox = fox_attention(seq(q_ext), seq(k_ext), proj3, COL_FOX // HEAD_DIM + 2 * nblk,
                          cb, thr, tile)
    o_dsa = dsa_attention(seq(iq), seq(iw), seq(ik), seq(qc), seq(kc), proj3,
                          COL_DSA_V // (N_KV_DSA * HEAD_DIM), topk)

    merged = merge_branches(h, o_sb.reshape(T, -1), o_fox.reshape(T, -1), o_dsa.reshape(T, -1),
                            w_gate.astype(BF16), w_branch.astype(BF16))
    x2 = matmul_residual(merged, w_out.astype(BF16), x2, tile=(1024, 1024))
    u = ffn_up(x2, norm_ffn_g, w_ffn_gate.astype(BF16), w_ffn_up.astype(BF16))
    return matmul_residual(u, w_ffn_down.astype(BF16), x2)


def kernel(x, norm_mix_g, w_in, fox_f_bias, fox_q_g, fox_k_g, dsa_q_g, dsa_k_g, w_gate, w_branch,
           w_out, norm_ffn_g, w_ffn_gate, w_ffn_up, w_ffn_down):
    B, S, D = x.shape
    topk = min(TOPK_MAX, S // 4)
    rope_h = _rope_tables(S, HEAD_DIM)
    rope_i = _rope_tables(S, IDX_DIM)
    place = _bias_placement()
    x2 = x.reshape(B * S, D)
    for l in range(norm_mix_g.shape[0]):
        x2 = _layer(x2, B, S, rope_h, rope_i, place, topk, norm_mix_g[l], w_in[l], fox_f_bias[l],
                    fox_q_g[l], fox_k_g[l], dsa_q_g[l], dsa_k_g[l], w_gate[l], w_branch[l],
                    w_out[l], norm_ffn_g[l], w_ffn_gate[l], w_ffn_up[l], w_ffn_down[l])
    return x2.reshape(B, S, D)
```

```python
import functools

import jax
import jax.numpy as jnp
import numpy as np
from jax import lax
from jax.experimental import pallas as pl
from jax.experimental.pallas import tpu as pltpu

F32 = jnp.float32
BF16 = jnp.bfloat16

HEAD_DIM = 128
N_HEADS = 8
N_KV_DSA = 2
IDX_HEADS = 16
IDX_DIM = 64
TOPK_MAX = 256
ROPE_THETA = 10000.0
NORM_EPS = 1e-6
BRANCH_W = N_HEADS * HEAD_DIM
MISC_W = 128
FOXF_LANE = IDX_DIM + IDX_HEADS
COL_SB = 0
COL_FOX = 3 * BRANCH_W
COL_DSA_Q = 6 * BRANCH_W
COL_IDX_Q = 7 * BRANCH_W
COL_DSA_K = 8 * BRANCH_W
COL_DSA_V = COL_DSA_K + N_KV_DSA * HEAD_DIM
PROJ_W = COL_DSA_V + N_KV_DSA * HEAD_DIM
PREP_TM = 256

LOG2E = 1.4426950408889634
SB_T, SB_SUB = 256, 8
FOX_TQ, FOX_TK = 1024, 512
DSA_TQ, DSA_TK = 256, 512
IDX_GROUP = 4
NEG = -1e30
INT_MIN = -(2 ** 31)
SB_SKIP = 100.0
FOX_SKIP = 152.0
VMEM_LIMIT_BYTES = 56 * 1024 * 1024


def _cparams(sem):
    return pltpu.CompilerParams(dimension_semantics=sem, vmem_limit_bytes=VMEM_LIMIT_BYTES)


def _tile(n, pref):
    t = pref
    while n % t:
        t //= 2
    return t


def _dot(a, b):
    return jnp.dot(a, b, preferred_element_type=F32)


def _dot_nt(a, b):
    return lax.dot_general(a, b, (((1,), (1,)), ((), ())), preferred_element_type=F32)


def _rep(x, n):
    return x if n == 1 else jnp.concatenate([x] * n, axis=1)


def _rmsnorm_f32(x, g):
    return x * lax.rsqrt(jnp.mean(x * x, axis=-1, keepdims=True) + NORM_EPS) * g


def _trunc_bf16(x):
    bits = lax.bitcast_convert_type(x, jnp.uint32) & jnp.uint32(0xFFFF0000)
    return lax.bitcast_convert_type(bits, F32)


def _norm_proj_kernel(x_ref, g_ref, w_ref, wmh_ref, wml_ref, proj_ref, h_ref, misc_ref, hs_ref):
    @pl.when(pl.program_id(1) == 0)
    def _():
        y = _rmsnorm_f32(x_ref[...], g_ref[...])
        hb = y.astype(BF16)
        hs_ref[...] = hb
        h_ref[...] = hb
        y_hi = _trunc_bf16(y)
        hi = y_hi.astype(BF16)
        lo = (y - y_hi).astype(BF16)
        misc_ref[...] = (_dot(hi, wmh_ref[...]) + _dot(hi, wml_ref[...])) + _dot(lo, wmh_ref[...])

    proj_ref[...] = _dot(hs_ref[...], w_ref[...]).astype(proj_ref.dtype)


def norm_proj(x, g, w, wm_hi, wm_lo):
    T, D = x.shape
    N = w.shape[1]
    tm, tn = _tile(T, 1024), _tile(N, 512)
    return pl.pallas_call(
        _norm_proj_kernel,
        grid=(T // tm, N // tn),
        in_specs=[
            pl.BlockSpec((tm, D), lambda i, j: (i, 0)),
            pl.BlockSpec((1, D), lambda i, j: (0, 0)),
            pl.BlockSpec((D, tn), lambda i, j: (0, j)),
            pl.BlockSpec((D, MISC_W), lambda i, j: (0, 0)),
            pl.BlockSpec((D, MISC_W), lambda i, j: (0, 0)),
        ],
        out_specs=[
            pl.BlockSpec((tm, tn), lambda i, j: (i, j)),
            pl.BlockSpec((tm, D), lambda i, j: (i, 0)),
            pl.BlockSpec((tm, MISC_W), lambda i, j: (i, 0)),
        ],
        out_shape=[
            jax.ShapeDtypeStruct((T, N), BF16),
            jax.ShapeDtypeStruct((T, D), BF16),
            jax.ShapeDtypeStruct((T, MISC_W), F32),
        ],
        scratch_shapes=[pltpu.VMEM((tm, D), BF16)],
        compiler_params=_cparams(("parallel", "arbitrary")),
        name="norm_proj",
    )(x, g.reshape(1, D), w, wm_hi, wm_lo)


def _merge_kernel(h_ref, o0_ref, o1_ref, o2_ref, wg_ref, wb_ref, out_ref):
    h = h_ref[...]
    acc = None
    for i, o_ref in enumerate((o0_ref, o1_ref, o2_ref)):
        y = jax.nn.sigmoid(_dot(h, wg_ref[i])) * _dot(o_ref[...], wb_ref[i])
        acc = y if acc is None else acc + y
    out_ref[...] = acc.astype(out_ref.dtype)


def merge_branches(h, o_sb, o_fox, o_dsa, wg, wb):
    T, D = h.shape
    W = o_sb.shape[1]
    N = wg.shape[2]
    tm, tn = _tile(T, 512), _tile(N, 512)
    o_spec = pl.BlockSpec((tm, W), lambda i, j: (i, 0))
    return pl.pallas_call(
        _merge_kernel,
        grid=(T // tm, N // tn),
        in_specs=[
            pl.BlockSpec((tm, D), lambda i, j: (i, 0)),
            o_spec, o_spec, o_spec,
            pl.BlockSpec((3, D, tn), lambda i, j: (0, 0, j)),
            pl.BlockSpec((3, W, tn), lambda i, j: (0, 0, j)),
        ],
        out_specs=pl.BlockSpec((tm, tn), lambda i, j: (i, j)),
        out_shape=jax.ShapeDtypeStruct((T, N), BF16),
        compiler_params=_cparams(("parallel", "arbitrary")),
        name="merge_branches",
    )(h, o_sb, o_fox, o_dsa, wg, wb)


def _matmul_residual_kernel(a_ref, b_ref, r_ref, o_ref):
    o_ref[...] = r_ref[...] + _dot(a_ref[...], b_ref[...])


def matmul_residual(a, b, r, tile=(512, 512)):
    T, K = a.shape
    N = b.shape[1]
    tm, tn = _tile(T, tile[0]), _tile(N, tile[1])
    return pl.pallas_call(
        _matmul_residual_kernel,
        grid=(T // tm, N // tn),
        in_specs=[
            pl.BlockSpec((tm, K), lambda i, j: (i, 0)),
            pl.BlockSpec((K, tn), lambda i, j: (0, j)),
            pl.BlockSpec((tm, tn), lambda i, j: (i, j)),
        ],
        out_specs=pl.BlockSpec((tm, tn), lambda i, j: (i, j)),
        out_shape=jax.ShapeDtypeStruct((T, N), F32),
        compiler_params=_cparams(("parallel", "arbitrary")),
        name="matmul_residual",
    )(a, b, r)


def _ffn_up_kernel(x_ref, g_ref, wg_ref, wu_ref, u_ref, hs_ref):
    @pl.when(pl.program_id(1) == 0)
    def _():
        hs_ref[...] = _rmsnorm_f32(x_ref[...], g_ref[...]).astype(BF16)

    h = hs_ref[...]
    u_ref[...] = (jax.nn.silu(_dot(h, wg_ref[...])) * _dot(h, wu_ref[...])).astype(u_ref.dtype)


def ffn_up(x, g, wg, wu):
    T, D = x.shape
    Fd = wg.shape[1]
    tm, tn = _tile(T, 1024), _tile(Fd, 512)
    return pl.pallas_call(
        _ffn_up_kernel,
        grid=(T // tm, Fd // tn),
        in_specs=[
            pl.BlockSpec((tm, D), lambda i, j: (i, 0)),
            pl.BlockSpec((1, D), lambda i, j: (0, 0)),
            pl.BlockSpec((D, tn), lambda i, j: (0, j)),
            pl.BlockSpec((D, tn), lambda i, j: (0, j)),
        ],
        out_specs=pl.BlockSpec((tm, tn), lambda i, j: (i, j)),
        out_shape=jax.ShapeDtypeStruct((T, Fd), BF16),
        scratch_shapes=[pltpu.VMEM((tm, D), BF16)],
        compiler_params=_cparams(("parallel", "arbitrary")),
        name="ffn_up",
    )(x, g.reshape(1, D), wg, wu)


def _online_softmax_update(s, v, m_ref, l_ref, acc_ref, idx):
    m_prev = m_ref[idx]
    m_new = jnp.maximum(m_prev, jnp.max(s, axis=1, keepdims=True))
    p = jnp.exp2(s - _rep(m_new, s.shape[1] // 128))
    alpha = jnp.exp2(m_prev - m_new)
    l_ref[idx] = alpha * l_ref[idx] + jnp.sum(p, axis=1, keepdims=True)
    acc_ref[idx] = alpha * acc_ref[idx] + _dot(p.astype(BF16), v)
    m_ref[idx] = m_new


def _sb_kernel(q_ref, k_ref, v_ref, o_ref, acc_ref, car_ref, *, t, n_sub):
    base = pl.program_id(2) * n_sub
    row = lax.broadcasted_iota(jnp.int32, (t, t), 0)
    col = lax.broadcasted_iota(jnp.int32, (t, t), 1)
    upper = (row > col).astype(BF16)
    ones = jnp.ones((t, 128), BF16)
    strict = col < row

    def process(g, j, diag, valid):
        start = pl.multiple_of(j * t, t)
        k = k_ref[0, pl.ds(start, t), :]
        v = v_ref[0, pl.ds(start, t), :]
        z = _dot_nt(q_ref[0, g * t:(g + 1) * t, :], k)
        log_keep = -(jnp.maximum(z, 0.0) + jnp.log(1.0 + jnp.exp(-jnp.abs(z))))
        lk = jnp.where(strict, log_keep, 0.0) if diag else log_keep
        hi_f = _trunc_bf16(lk)
        hi = hi_f.astype(BF16)
        lo = (lk - hi_f).astype(BF16)
        after = _dot(hi, upper) + _dot(lo, upper)
        total = _dot(hi, ones) + _dot(lo, ones)
        car = car_ref[g]
        a = jnp.exp(z + log_keep + after + _rep(car, t // 128))
        if diag:
            a = jnp.where(strict, a, 0.0)
        if valid is not None:
            a = jnp.where(valid, a, 0.0)
            total = jnp.where(valid, total, 0.0)
        acc_ref[g] += _dot(a.astype(BF16), v)
        car = car + total
        car_ref[g] = car
        return jnp.max(car)

    acc_ref[...] = jnp.zeros_like(acc_ref)
    car_ref[...] = jnp.zeros_like(car_ref)
    peak = [process(g, base + g, True, None) for g in range(n_sub)]

    def more(depth, peak):
        need = [jnp.logical_and(base + g - depth >= 0, peak[g] > -SB_SKIP) for g in range(n_sub)]
        return functools.reduce(jnp.logical_or, need)

    def body(st):
        depth, _ = st
        peak = []
        for g in range(n_sub):
            j = base + g - depth
            peak.append(process(g, jnp.maximum(j, 0), False, j >= 0))
        return depth + 1, more(depth + 1, peak)

    lax.while_loop(lambda st: st[1], body, (jnp.int32(1), more(1, peak)))
    for g in range(n_sub):
        o_ref[0, g * t:(g + 1) * t, :] = acc_ref[g].astype(o_ref.dtype)


def sb_attention(proj3, q_col, k_col, v_col):
    B, S, _ = proj3.shape
    t = _tile(S, SB_T)
    n_sub = _tile(S // t, SB_SUB)
    rows = t * n_sub
    return pl.pallas_call(
        functools.partial(_sb_kernel, t=t, n_sub=n_sub),
        grid=(B, N_HEADS, S // rows),
        in_specs=[
            pl.BlockSpec((1, rows, HEAD_DIM), lambda b, h, i: (b, i, q_col + h)),
            pl.BlockSpec((1, S, HEAD_DIM), lambda b, h, i: (b, 0, k_col + h)),
            pl.BlockSpec((1, S, HEAD_DIM), lambda b, h, i: (b, 0, v_col + h)),
        ],
        out_specs=pl.BlockSpec((1, rows, HEAD_DIM), lambda b, h, i: (b, i, h)),
        out_shape=jax.ShapeDtypeStruct((B, S, BRANCH_W), BF16),
        scratch_shapes=[pltpu.VMEM((n_sub, t, HEAD_DIM), F32), pltpu.VMEM((n_sub, t, HEAD_DIM), F32)],
        compiler_params=_cparams(("parallel", "parallel", "arbitrary")),
        name="sb_attention",
    )(proj3, proj3, proj3)


def _fox_kernel(cb_ref, thr_ref, q_ref, k_ref, v_ref, o_ref, m_ref, l_ref, acc_ref,
                *, tq, tk, tile):
    i = pl.program_id(2)
    q = q_ref[0]
    m_ref[...] = jnp.full_like(m_ref, NEG)
    l_ref[...] = jnp.zeros_like(l_ref)
    acc_ref[...] = jnp.zeros_like(acc_ref)
    row = lax.broadcasted_iota(jnp.int32, (tq, tk), 0)
    col = lax.broadcasted_iota(jnp.int32, (tq, tk), 1)

    def step(start, mask):
        start = pl.multiple_of(start, tk)
        s = _dot_nt(q, k_ref[0, pl.ds(start, tk), :])
        if mask is not None:
            s = jnp.where(mask, s, NEG)
        _online_softmax_update(s, v_ref[0, pl.ds(start, tk), :], m_ref, l_ref, acc_ref, 0)

    def body(j, c):
        step(j * tk, None)
        return c

    head = pl.program_id(0) * N_HEADS + pl.program_id(1)
    c_first = cb_ref[head, 2 * (i * (tq // tile))]
    n_before = i * (tq // tk)

    def negligible(j):
        c_last = cb_ref[head, 2 * ((j + 1) * (tk // tile) - 1) + 1]
        return jnp.logical_and(j < n_before, c_first - c_last < -thr_ref[0])

    first = lax.while_loop(negligible, lambda j: j + 1, jnp.int32(0))
    lax.fori_loop(first, n_before, body, 0)
    for d in range(tq // tk):
        step(i * tq + d * tk, row >= col + d * tk)
    o_ref[0] = (acc_ref[0] / l_ref[0]).astype(o_ref.dtype)


def fox_attention(q_ext, k_ext, proj3, v_col, cb, thr, tile):
    B, S, _ = q_ext.shape
    tq = _tile(S, FOX_TQ)
    tk = _tile(tq, FOX_TK)
    E = 2 * HEAD_DIM
    smem = pl.BlockSpec(memory_space=pltpu.SMEM)
    return pl.pallas_call(
        functools.partial(_fox_kernel, tq=tq, tk=tk, tile=tile),
        grid=(B, N_HEADS, S // tq),
        in_specs=[
            smem, smem,
            pl.BlockSpec((1, tq, E), lambda b, h, i: (b, i, h)),
            pl.BlockSpec((1, S, E), lambda b, h, i: (b, 0, h)),
            pl.BlockSpec((1, S, HEAD_DIM), lambda b, h, i: (b, 0, v_col + h)),
        ],
        out_specs=pl.BlockSpec((1, tq, HEAD_DIM), lambda b, h, i: (b, i, h)),
        out_shape=jax.ShapeDtypeStruct((B, S, BRANCH_W), BF16),
        scratch_shapes=[pltpu.VMEM((1, tq, HEAD_DIM), F32)] * 3,
        compiler_params=_cparams(("parallel", "parallel", "arbitrary")),
        name="fox_attention",
    )(cb, thr, q_ext, k_ext, proj3)


def _dsa_kernel(iq_ref, w_ref, ik_ref, qc_ref, kc_ref, vc_ref, o_ref,
                iqs_ref, key_ref, k16_ref, m_ref, l_ref, acc_ref, *, tq, tk, topk):
    i = pl.program_id(1)
    nkb = (i * tq + tq + tk - 1) // tk
    rep = N_HEADS // N_KV_DSA
    kf = float(topk)

    w_t = w_ref[0].T
    for h in range(IDX_HEADS):
        iqs_ref[h * tq:(h + 1) * tq, :] = iq_ref[0, :, h * IDX_DIM:(h + 1) * IDX_DIM]
    qpos = i * tq + lax.broadcasted_iota(jnp.int32, (tk, tq), 1)
    krow = lax.broadcasted_iota(jnp.int32, (tk, tq), 0)

    def score_block(jk, c):
        start = pl.multiple_of(jk * tk, tk)
        ikb = ik_ref[0, pl.ds(start, tk), :]
        score = jnp.zeros((tk, tq), F32)
        for hg in range(IDX_HEADS // IDX_GROUP):
            z = _dot_nt(ikb, iqs_ref[hg * IDX_GROUP * tq:(hg + 1) * IDX_GROUP * tq, :])
            for r in range(IDX_GROUP):
                h = hg * IDX_GROUP + r
                score = score + jnp.maximum(z[:, r * tq:(r + 1) * tq], 0.0) * w_t[h:h + 1, :]
        bits = lax.bitcast_convert_type(score, jnp.int32)
        key = bits ^ ((bits >> 31) & jnp.int32(0x7FFFFFFF))
        key = jnp.where(start + krow <= qpos, key, jnp.int32(INT_MIN))
        key_ref[jk] = key
        k16_ref[jk] = (key >> 16).astype(jnp.int16)
        return c

    lax.fori_loop(0, nkb, score_block, 0)

    def count_ge(cand):
        cw = cand.astype(jnp.int16)

        def blk(jk, c):
            hit = jnp.where(k16_ref[jk] >= cw, jnp.int16(1), jnp.int16(0))
            for r in range(tk // 16):
                c = c + hit[r * 16:(r + 1) * 16, :]
            return c

        c = lax.fori_loop(0, nkb, blk, jnp.zeros((16, tq), jnp.int16))
        return jnp.sum(c.astype(jnp.int32).astype(F32), axis=0, keepdims=True)

    def bisect16(need, count_all):
        def bit(it, st):
            tau, cge = st
            cand = tau + lax.shift_left(jnp.int32(1), 15 - it)
            cnt = count_ge(cand)
            ok = cnt >= need
            return jnp.where(ok, cand, tau), jnp.where(ok, cnt, cge)

        return lax.fori_loop(0, 16, bit, (jnp.full((1, tq), -32768, jnp.int32), count_all))

    n_all = jnp.zeros((1, tq), F32) + (nkb * tk).astype(F32)
    tau_hi, cge_hi = bisect16(kf, n_all)
    n_gt = count_ge(tau_hi + 1)
    t16 = tau_hi.astype(jnp.int16)

    def low_block(jk, c):
        lo = ((key_ref[jk] & jnp.int32(0xFFFF)) - 32768).astype(jnp.int16)
        k16_ref[jk] = jnp.where(k16_ref[jk] == t16, lo, jnp.int16(-32768))
        return c

    lax.fori_loop(0, nkb, low_block, 0)
    tau_lo, cge_lo = bisect16(kf - n_gt, cge_hi - n_gt)
    tau = jnp.maximum(lax.shift_left(tau_hi, 16) | (tau_lo + 32768), jnp.int32(INT_MIN + 1))

    surplus = (n_gt + cge_lo) - kf

    @pl.when(jnp.max(surplus) > 0.0)
    def _():
        def count(pred):
            def blk(jk, c):
                hit = jnp.where(pred(key_ref[jk], jk * tk + krow), 1.0, 0.0)
                return c + jnp.sum(hit, axis=0, keepdims=True)

            return lax.fori_loop(0, nkb, blk, jnp.zeros((1, tq), F32))

        keep = kf - count(lambda key, pos: key > tau)
        nbits = (key_ref.shape[0] * tk - 1).bit_length()

        def bit(it, last):
            cand = last + lax.shift_left(jnp.int32(1), nbits - 1 - it)
            before = count(lambda key, pos: jnp.logical_and(key == tau, pos < cand))
            return jnp.where(before < keep, cand, last)

        last = lax.fori_loop(0, nbits, bit, jnp.zeros((1, tq), jnp.int32))
        last = jnp.where(surplus > 0.0, last, jnp.int32(2 ** 30))

        def drop(jk, c):
            key = key_ref[jk]
            late_tie = jnp.logical_and(key == tau, jk * tk + krow > last)
            key_ref[jk] = jnp.where(late_tie, jnp.int32(INT_MIN), key)
            return c

        lax.fori_loop(0, nkb, drop, 0)

    m_ref[...] = jnp.full_like(m_ref, NEG)
    l_ref[...] = jnp.zeros_like(l_ref)
    acc_ref[...] = jnp.zeros_like(acc_ref)
    qc = qc_ref[0]
    qg = [jnp.concatenate([qc[:, (rep * g + r) * HEAD_DIM:(rep * g + r + 1) * HEAD_DIM]
                           for r in range(rep)], axis=0) for g in range(N_KV_DSA)]

    def attend_block(jk, c):
        start = pl.multiple_of(jk * tk, tk)
        bias = jnp.where(key_ref[jk] >= tau, 0.0, NEG).T
        bias = jnp.concatenate([bias] * rep, axis=0)
        for g in range(N_KV_DSA):
            kg = kc_ref[0, pl.ds(start, tk), g * HEAD_DIM:(g + 1) * HEAD_DIM]
            vg = vc_ref[0, pl.ds(start, tk), g * HEAD_DIM:(g + 1) * HEAD_DIM]
            _online_softmax_update(_dot_nt(qg[g], kg) + bias, vg, m_ref, l_ref, acc_ref, g)
        return c

    lax.fori_loop(0, nkb, attend_block, 0)
    for g in range(N_KV_DSA):
        og = (acc_ref[g] / l_ref[g]).astype(o_ref.dtype)
        for r in range(rep):
            hcol = (rep * g + r) * HEAD_DIM
            o_ref[0, :, hcol:hcol + HEAD_DIM] = og[r * tq:(r + 1) * tq]


def dsa_attention(iq, iw, ik, qc, kc, proj3, v_col, topk):
    B, S, _ = qc.shape
    tq = _tile(S, DSA_TQ)
    tk = _tile(S, DSA_TK)
    kvw = N_KV_DSA * HEAD_DIM
    rows = (N_HEADS // N_KV_DSA) * tq
    resident = pl.Buffered(1)
    return pl.pallas_call(
        functools.partial(_dsa_kernel, tq=tq, tk=tk, topk=topk),
        grid=(B, S // tq),
        in_specs=[
            pl.BlockSpec((1, tq, IDX_HEADS * IDX_DIM), lambda b, i: (b, i, 0)),
            pl.BlockSpec((1, tq, IDX_HEADS), lambda b, i: (b, i, 0)),
            pl.BlockSpec((1, S, IDX_DIM), lambda b, i: (b, 0, 0), pipeline_mode=resident),
            pl.BlockSpec((1, tq, BRANCH_W), lambda b, i: (b, i, 0)),
            pl.BlockSpec((1, S, kvw), lambda b, i: (b, 0, 0), pipeline_mode=resident),
            pl.BlockSpec((1, S, kvw), lambda b, i: (b, 0, v_col), pipeline_mode=resident),
        ],
        out_specs=pl.BlockSpec((1, tq, BRANCH_W), lambda b, i: (b, i, 0)),
        out_shape=jax.ShapeDtypeStruct((B, S, BRANCH_W), BF16),
        scratch_shapes=[
            pltpu.VMEM((IDX_HEADS * tq, IDX_DIM), BF16),
            pltpu.VMEM((S // tk, tk, tq), jnp.int32),
            pltpu.VMEM((S // tk, tk, tq), jnp.int16),
            pltpu.VMEM((N_KV_DSA, rows, HEAD_DIM), F32),
            pltpu.VMEM((N_KV_DSA, rows, HEAD_DIM), F32),
            pltpu.VMEM((N_KV_DSA, rows, HEAD_DIM), F32),
        ],
        compiler_params=_cparams(("parallel", "arbitrary")),
        name="dsa_attention",
    )(iq, iw, ik, qc, kc, proj3)


def _split3(c):
    c1 = _trunc_bf16(c)
    r = c - c1
    c2 = _trunc_bf16(r)
    return c1.astype(BF16), c2.astype(BF16), (r - c2).astype(BF16)


def _prep_kernel(fq_ref, fk_ref, dq_ref, iq_ref, dk_ref, misc_ref, const_ref, rope_h_ref,
                 rope_i_ref, pq_ref, pk_ref,
                 qext_ref, kext_ref, qc_ref, kc_ref, iqo_ref, ik_ref, iw_ref, cb_ref, carry_ref,
                 *, tm, tiles_per_seq):
    @pl.when(pl.program_id(0) % tiles_per_seq == 0)
    def _():
        carry_ref[...] = jnp.zeros_like(carry_ref)

    cos_h, sin_h = rope_h_ref[:, :128], rope_h_ref[:, 128:]
    cos_i, sin_i = rope_i_ref[:, :128], rope_i_ref[:, 128:]
    lane = lax.broadcasted_iota(jnp.int32, (tm, 128), 1)
    low_half = (lane & (IDX_DIM - 1)) < IDX_DIM // 2

    def head_norm(x, g):
        return x * lax.rsqrt(jnp.mean(x * x, axis=1, keepdims=True) + NORM_EPS) * g

    def rope128(x):
        return x * cos_h + pltpu.roll(x, 64, 1) * sin_h

    def rope64(x):
        swapped = jnp.where(low_half, pltpu.roll(x, 96, 1), pltpu.roll(x, 32, 1))
        return x * cos_i + swapped * sin_i

    g_fq, g_fk = const_ref[0:1, :], const_ref[1:2, :]
    g_dq, g_dk = const_ref[2:3, :], const_ref[3:4, :]
    misc = misc_ref[...]

    x = misc + const_ref[4:5, :]
    log_f = jnp.minimum(x, 0.0) - jnp.log(1.0 + jnp.exp(-jnp.abs(x)))
    row = lax.broadcasted_iota(jnp.int32, (tm, tm), 0)
    col = lax.broadcasted_iota(jnp.int32, (tm, tm), 1)
    tri = (row >= col).astype(BF16)
    f1, f2, f3 = _split3(log_f)
    c = (_dot(tri, f1) + _dot(tri, f2)) + _dot(tri, f3) + carry_ref[...]
    carry_ref[...] = c[tm - 1:tm, :]
    c = c * LOG2E
    cb_ref[0, 0:1, :] = c[0:1, :]
    cb_ref[0, 1:2, :] = c[tm - 1:tm, :]
    c1, c2, c3 = _split3(c)
    lane_w = lax.broadcasted_iota(jnp.int32, (tm, BRANCH_W), 1) & (HEAD_DIM - 1)
    qb = (_dot(c1, pq_ref[0]) + _dot(c2, pq_ref[1]) + _dot(c3, pq_ref[2])
          + jnp.where((lane_w >= 3) & (lane_w < 6), 1.0, 0.0))
    kb = (_dot(c1, pk_ref[0]) + _dot(c2, pk_ref[1]) + _dot(c3, pk_ref[2])
          + jnp.where(lane_w < 3, 1.0, 0.0))

    for h in range(N_HEADS):
        sl = slice(h * HEAD_DIM, (h + 1) * HEAD_DIM)
        e0, e1, e2 = 2 * h * HEAD_DIM, (2 * h + 1) * HEAD_DIM, (2 * h + 2) * HEAD_DIM
        qext_ref[:, e0:e1] = head_norm(fq_ref[:, sl].astype(F32), g_fq).astype(BF16)
        qext_ref[:, e1:e2] = qb[:, sl].astype(BF16)
        kext_ref[:, e0:e1] = head_norm(fk_ref[:, sl].astype(F32), g_fk).astype(BF16)
        kext_ref[:, e1:e2] = kb[:, sl].astype(BF16)
        qc_ref[:, sl] = rope128(head_norm(dq_ref[:, sl].astype(F32), g_dq)).astype(BF16)
        iqo_ref[:, sl] = rope64(iq_ref[:, sl].astype(F32)).astype(BF16)
    for g in range(N_KV_DSA):
        sl = slice(g * HEAD_DIM, (g + 1) * HEAD_DIM)
        kc_ref[:, sl] = rope128(head_norm(dk_ref[:, sl].astype(F32), g_dk)).astype(BF16)
    ik_ref[...] = rope64(misc)[:, :IDX_DIM].astype(BF16)
    iw_ref[...] = misc[:, IDX_DIM:IDX_DIM + IDX_HEADS] * (IDX_HEADS ** -0.5 * IDX_DIM ** -0.5)


def prep_attention_operands(proj, misc, consts, rope_h, rope_i, pq, pk, S):
    T = proj.shape[0]
    tm = _tile(S, PREP_TM)
    tps = S // tm
    wide = lambda c: pl.BlockSpec((tm, BRANCH_W), lambda i: (i, c))
    full = lambda a: pl.BlockSpec(a.shape, lambda i: (0,) * a.ndim)
    kvw = N_KV_DSA * HEAD_DIM
    return pl.pallas_call(
        functools.partial(_prep_kernel, tm=tm, tiles_per_seq=tps),
        grid=(T // tm,),
        in_specs=[
            wide(COL_FOX // BRANCH_W), wide(COL_FOX // BRANCH_W + 1),
            wide(COL_DSA_Q // BRANCH_W), wide(COL_IDX_Q // BRANCH_W),
            pl.BlockSpec((tm, kvw), lambda i: (i, COL_DSA_K // kvw)),
            pl.BlockSpec((tm, MISC_W), lambda i: (i, 0)),
            full(consts),
            pl.BlockSpec((tm, 256), lambda i: (i % tps, 0)),
            pl.BlockSpec((tm, 256), lambda i: (i % tps, 0)),
            full(pq), full(pk),
        ],
        out_specs=[
            pl.BlockSpec((tm, 2 * BRANCH_W), lambda i: (i, 0)),
            pl.BlockSpec((tm, 2 * BRANCH_W), lambda i: (i, 0)),
            pl.BlockSpec((tm, BRANCH_W), lambda i: (i, 0)),
            pl.BlockSpec((tm, kvw), lambda i: (i, 0)),
            pl.BlockSpec((tm, BRANCH_W), lambda i: (i, 0)),
            pl.BlockSpec((tm, IDX_DIM), lambda i: (i, 0)),
            pl.BlockSpec((tm, IDX_HEADS), lambda i: (i, 0)),
            pl.BlockSpec((1, 2, MISC_W), lambda i: (i, 0, 0)),
        ],
        out_shape=[
            jax.ShapeDtypeStruct((T, 2 * BRANCH_W), BF16),
            jax.ShapeDtypeStruct((T, 2 * BRANCH_W), BF16),
            jax.ShapeDtypeStruct((T, BRANCH_W), BF16),
            jax.ShapeDtypeStruct((T, kvw), BF16),
            jax.ShapeDtypeStruct((T, BRANCH_W), BF16),
            jax.ShapeDtypeStruct((T, IDX_DIM), BF16),
            jax.ShapeDtypeStruct((T, IDX_HEADS), F32),
            jax.ShapeDtypeStruct((T // tm, 2, MISC_W), F32),
        ],
        scratch_shapes=[pltpu.VMEM((1, MISC_W), F32)],
        compiler_params=_cparams(("arbitrary",)),
        name="prep_attention_operands",
    )(proj, proj, proj, proj, proj, misc, consts, rope_h, rope_i, pq, pk)


def _rope_tables(seq, dim):
    inv = 1.0 / (ROPE_THETA ** (jnp.arange(0, dim, 2, dtype=F32) / dim))
    ang = jnp.arange(seq, dtype=F32)[:, None] * inv[None, :]
    cos, sin = jnp.cos(ang), jnp.sin(ang)
    reps = 128 // dim
    return jnp.concatenate([jnp.tile(jnp.concatenate([cos, cos], axis=1), (1, reps)),
                            jnp.tile(jnp.concatenate([-sin, sin], axis=1), (1, reps))], axis=1)


def _bias_placement():
    pq = np.zeros((3, MISC_W, BRANCH_W), np.float32)
    pk = np.zeros((3, MISC_W, BRANCH_W), np.float32)
    for p in range(3):
        for h in range(N_HEADS):
            pq[p, FOXF_LANE + h, h * HEAD_DIM + p] = 1.0
            pk[p, FOXF_LANE + h, h * HEAD_DIM + 3 + p] = -1.0
    return jnp.asarray(pq, BF16), jnp.asarray(pk, BF16)


def _pack_in_weights(w_in):
    D = w_in.shape[0]
    scale = HEAD_DIM ** -0.5
    o = 0
    sb = w_in[:, o:o + 3 * BRANCH_W]; o += 3 * BRANCH_W
    fox = w_in[:, o:o + 3 * BRANCH_W]; o += 3 * BRANCH_W
    fox_f = w_in[:, o:o + N_HEADS]; o += N_HEADS
    dsa_q = w_in[:, o:o + BRANCH_W]; o += BRANCH_W
    dsa_kv = w_in[:, o:o + 2 * N_KV_DSA * HEAD_DIM]; o += 2 * N_KV_DSA * HEAD_DIM
    idx_q = w_in[:, o:o + IDX_HEADS * IDX_DIM]; o += IDX_HEADS * IDX_DIM
    idx_k = w_in[:, o:o + IDX_DIM]; o += IDX_DIM
    idx_w = w_in[:, o:o + IDX_HEADS]; o += IDX_HEADS
    sb = jnp.concatenate([sb[:, :BRANCH_W] * scale, sb[:, BRANCH_W:]], axis=1)
    w_main = jnp.concatenate([sb, fox, dsa_q, idx_q, dsa_kv], axis=1).astype(BF16)
    pad = jnp.zeros((D, MISC_W - IDX_DIM - IDX_HEADS - N_HEADS), F32)
    misc = jnp.concatenate([idx_k, idx_w, fox_f, pad], axis=1)
    m_hi = _trunc_bf16(misc)
    return w_main, m_hi.astype(BF16), (misc - m_hi).astype(BF16)


def _layer(x2, B, S, rope_h, rope_i, place, topk, norm_mix_g, w_in, fox_f_bias, fox_q_g, fox_k_g,
           dsa_q_g, dsa_k_g, w_gate, w_branch, w_out, norm_ffn_g, w_ffn_gate, w_ffn_up, w_ffn_down):
    T, D = x2.shape
    qscale = HEAD_DIM ** -0.5 * LOG2E
    w_main, wm_hi, wm_lo = _pack_in_weights(w_in)
    proj, h, misc = norm_proj(x2, norm_mix_g, w_main, wm_hi, wm_lo)

    fbias = jnp.zeros((MISC_W,), F32).at[FOXF_LANE:FOXF_LANE + N_HEADS].set(fox_f_bias)
    consts = jnp.zeros((8, MISC_W), F32)
    consts = consts.at[0].set(fox_q_g * qscale).at[1].set(fox_k_g)
    consts = consts.at[2].set(dsa_q_g * qscale).at[3].set(dsa_k_g).at[4].set(fbias)
    q_ext, k_ext, qc, kc, iq, ik, iw, cb = prep_attention_operands(
        proj, misc, consts, rope_h, rope_i, place[0], place[1], S)
    tile = _tile(S, PREP_TM)
    cb = cb.reshape(B, S // tile, 2, MISC_W)[..., FOXF_LANE:FOXF_LANE + N_HEADS]
    cb = cb.transpose(0, 3, 1, 2).reshape(B * N_HEADS, 2 * (S // tile))
    zmax = 1.02 * HEAD_DIM * jnp.max(jnp.abs(fox_q_g * qscale)) * jnp.max(jnp.abs(fox_k_g))
    thr = (2.0 * zmax + FOX_SKIP).reshape(1)

    proj3 = proj.reshape(B, S, PROJ_W)
    nblk = BRANCH_W // HEAD_DIM
    seq = lambda a: a.reshape(B, S, a.shape[-1])

    o_sb = sb_attention(proj3, COL_SB // HEAD_DIM, COL_SB // HEAD_DIM + nblk,
                        COL_SB // HEAD_DIM + 2 * nblk)
    o_fox = fox_attention(seq(q_ext), seq(k_ext), proj3, COL_FOX // HEAD_DIM + 2 * nblk,
                          cb, thr, tile)
    o_dsa = dsa_attention(seq(iq), seq(iw), seq(ik), seq(qc), seq(kc), proj3,
                          COL_DSA_V // (N_KV_DSA * HEAD_DIM), topk)

    merged = merge_branches(h, o_sb.reshape(T, -1), o_fox.reshape(T, -1), o_dsa.reshape(T, -1),
                            w_gate.astype(BF16), w_branch.astype(BF16))
    x2 = matmul_residual(merged, w_out.astype(BF16), x2, tile=(1024, 1024))
    u = ffn_up(x2, norm_ffn_g, w_ffn_gate.astype(BF16), w_ffn_up.astype(BF16))
    return matmul_residual(u, w_ffn_down.astype(BF16), x2)


def kernel(x, norm_mix_g, w_in, fox_f_bias, fox_q_g, fox_k_g, dsa_q_g, dsa_k_g, w_gate, w_branch,
           w_out, norm_ffn_g, w_ffn_gate, w_ffn_up, w_ffn_down):
    B, S, D = x.shape
    topk = min(TOPK_MAX, S // 4)
    rope_h = _rope_tables(S, HEAD_DIM)
    rope_i = _rope_tables(S, IDX_DIM)
    place = _bias_placement()
    x2 = x.reshape(B * S, D)
    for l in range(norm_mix_g.shape[0]):
        x2 = _layer(x2, B, S, rope_h, rope_i, place, topk, norm_mix_g[l], w_in[l], fox_f_bias[l],
                    fox_q_g[l], fox_k_g[l], dsa_q_g[l], dsa_k_g[l], w_gate[l], w_branch[l],
                    w_out[l], norm_ffn_g[l], w_ffn_gate[l], w_ffn_up[l], w_ffn_down[l])
    return x2.reshape(B, S, D)
```
